```python
import jax, jax.numpy as jnp
from jax import lax
import numpy as np

D_MODEL = 1024
BATCH = 8
SEQ = 4096
DEPTH = 2
DEC_BATCH = 32
DEC_SEQ = 4
PAST_LEN = 16384
PAGE_SIZE = 128

N_HEADS = 16
HEAD_DIM = D_MODEL // N_HEADS
D_FF = ((8 * D_MODEL // 3 + 255) // 256) * 256
N_META = 16
CONV_W = 3
Q_BLOCK = 128
NORM_EPS = 1e-6
N_MIXERS = 2
N_ATTN_LAYERS = (DEPTH + N_MIXERS - 1) // N_MIXERS
N_CONV_LAYERS = DEPTH // N_MIXERS
FORGET_BIAS = 3.0
HALF = 0.5

kernel_name = 'hybrid_fox_shortconv_macaron_step'


def rmsnorm(x, g):
    xf = x.astype(jnp.float32)
    y = xf * lax.rsqrt(jnp.mean(xf * xf, axis=-1, keepdims=True) + NORM_EPS)
    return (y * g.astype(jnp.float32)).astype(x.dtype)


def swiglu(x, wg, wu, wd):
    return (jax.nn.silu(x @ wg) * (x @ wu)) @ wd


def half_ffn(x, g_pre, g_post, wg, wu, wd):
    return x + HALF * rmsnorm(swiglu(rmsnorm(x, g_pre), wg, wu, wd), g_post)


def fox_project(h, w_in, b_f):
    n, t, _ = h.shape
    proj = h @ w_in
    qkv = proj[..., :3 * D_MODEL].reshape(n, t, 3, N_HEADS, HEAD_DIM)
    logf = jax.nn.log_sigmoid((proj[..., 3 * D_MODEL:] + b_f).astype(jnp.float32))
    return qkv[:, :, 0], qkv[:, :, 1], qkv[:, :, 2], logf


def fox_prompt(q, k, v, logf):
    n, L, H, Dh = q.shape
    n_real = L - N_META
    nb = n_real // Q_BLOCK
    scale = HEAD_DIM ** -0.5
    qh = q.transpose(0, 2, 1, 3)
    kh = k.transpose(0, 2, 1, 3)
    vh = v.transpose(0, 2, 1, 3)
    Fh = jnp.cumsum(logf, axis=1).transpose(0, 2, 1)
    key_pos = jnp.arange(L)

    def attend(blk):
        q_blk, F_q, t_q = blk
        s = jnp.einsum('bhqd,bhkd->bhqk', q_blk, kh, preferred_element_type=jnp.float32) * scale
        s = s + F_q[..., :, None] - Fh[..., None, :]
        s = jnp.where(t_q[:, None] >= key_pos[None, :], s, -jnp.inf)
        p = jax.nn.softmax(s, axis=-1).astype(vh.dtype)
        return jnp.einsum('bhqk,bhkd->bhqd', p, vh)

    out_meta = attend((qh[:, :, :N_META], Fh[:, :, :N_META], key_pos[:N_META]))
    q_real = qh[:, :, N_META:].reshape(n, H, nb, Q_BLOCK, Dh).transpose(2, 0, 1, 3, 4)
    F_real = Fh[:, :, N_META:].reshape(n, H, nb, Q_BLOCK).transpose(2, 0, 1, 3)
    t_real = key_pos[N_META:].reshape(nb, Q_BLOCK)
    out_real = lax.map(attend, (q_real, F_real, t_real))
    out_real = out_real.transpose(1, 0, 3, 2, 4).reshape(n, n_real, H, Dh)
    return jnp.concatenate([out_meta.transpose(0, 2, 1, 3), out_real], axis=1)


def fox_sample(q, k, v, logf, cache_k, cache_v, cache_logf, page_table, layer):
    T = q.shape[1]
    scale = HEAD_DIM ** -0.5
    causal = jnp.tril(jnp.ones((T, T), dtype=bool))

    def one(args):
        q_b, k_b, v_b, lf_b, pages = args
        k_p = cache_k[layer, pages].reshape(-1, N_HEADS, HEAD_DIM)
        v_p = cache_v[layer, pages].reshape(-1, N_HEADS, HEAD_DIM)
        lf_p = cache_logf[layer, pages].reshape(-1, N_HEADS).astype(jnp.float32)
        P = k_p.shape[0]
        G = lax.cumsum(lf_p, axis=0, reverse=True) - lf_p
        Fn = jnp.cumsum(lf_b, axis=0).T
        s_past = jnp.einsum('qhd,khd->hqk', q_b, k_p, preferred_element_type=jnp.float32) * scale
        s_past = s_past + Fn[:, :, None] + G.T[:, None, :]
        s_new = jnp.einsum('qhd,khd->hqk', q_b, k_b, preferred_element_type=jnp.float32) * scale
        s_new = jnp.where(causal, s_new + Fn[:, :, None] - Fn[:, None, :], -jnp.inf)
        p = jax.nn.softmax(jnp.concatenate([s_past, s_new], axis=-1), axis=-1).astype(v_b.dtype)
        return (jnp.einsum('hqk,khd->qhd', p[..., :P], v_p)
                + jnp.einsum('hqk,khd->qhd', p[..., P:], v_b))

    return lax.map(one, (q, k, v, logf, page_table))


def conv_project(h, w_in):
    b, c, hh = jnp.split(h @ w_in, 3, axis=-1)
    return b, c * hh


def short_conv(u_ext, w):
    T = u_ext.shape[1] - (CONV_W - 1)
    out = w[0] * u_ext[:, 0:T]
    for j in range(1, CONV_W):
        out = out + w[j] * u_ext[:, j:j + T]
    return out


def setup_inputs(seed: int = 0) -> dict:
    key = jax.random.key(seed)
    ks = jax.random.split(key, 18)
    n_pages = PAST_LEN // PAGE_SIZE
    n_used = DEC_BATCH * n_pages
    n_pool = (5 * n_used + 3) // 4
    D, H, Dh = D_MODEL, N_HEADS, HEAD_DIM
    f32 = jnp.float32

    def w(k, shape, fan_in):
        return jax.random.normal(k, shape, f32) * fan_in ** -0.5

    return {
        'x_prompt': jax.random.normal(ks[0], (BATCH, SEQ, D), f32),
        'x_sample': jax.random.normal(ks[1], (DEC_BATCH, DEC_SEQ, D), f32),
        'cache_k': jax.random.normal(ks[2], (N_ATTN_LAYERS, n_pool, PAGE_SIZE, H, Dh), f32),
        'cache_v': jax.random.normal(ks[3], (N_ATTN_LAYERS, n_pool, PAGE_SIZE, H, Dh), f32),
        'cache_logf': jax.nn.log_sigmoid(FORGET_BIAS + 0.5 * jax.random.normal(ks[4], (N_ATTN_LAYERS, n_pool, PAGE_SIZE, H), f32)),
        'state_conv': jax.random.normal(ks[5], (N_CONV_LAYERS, DEC_BATCH, CONV_W - 1, D), f32),
        'page_table': jax.random.permutation(ks[6], n_pool)[:n_used].reshape(DEC_BATCH, n_pages).astype(jnp.int32),
        'meta_tokens': jax.random.normal(ks[7], (N_META, D), f32),
        'norm_g': 1.0 + 0.05 * jax.random.normal(ks[8], (DEPTH, 6, D), f32),
        'ffn_w_gate': w(ks[9], (DEPTH, 2, D, D_FF), D),
        'ffn_w_up': w(ks[10], (DEPTH, 2, D, D_FF), D),
        'ffn_w_down': w(ks[11], (DEPTH, 2, D_FF, D), D_FF),
        'attn_w_in': w(ks[12], (N_ATTN_LAYERS, D, 3 * D + H), D),
        'attn_b_f': FORGET_BIAS + 0.5 * jax.random.normal(ks[13], (N_ATTN_LAYERS, H), f32),
        'attn_w_out': w(ks[14], (N_ATTN_LAYERS, D, D), D),
        'conv_w_in': w(ks[15], (N_CONV_LAYERS, D, 3 * D), D),
        'conv_kernel': w(ks[16], (N_CONV_LAYERS, CONV_W, D), CONV_W),
        'conv_w_out': w(ks[17], (N_CONV_LAYERS, D, D), D),
    }


def reference(x_prompt, x_sample, cache_k, cache_v, cache_logf, state_conv, page_table,
              meta_tokens, norm_g, ffn_w_gate, ffn_w_up, ffn_w_down,
              attn_w_in, attn_b_f, attn_w_out, conv_w_in, conv_kernel, conv_w_out):
    n_prompt = x_prompt.shape[0]
    meta = jnp.broadcast_to(meta_tokens[None].astype(x_prompt.dtype), (n_prompt, N_META, D_MODEL))
    xp = jnp.concatenate([meta, x_prompt], axis=1)
    xs = x_sample
    k_p_l, v_p_l, lf_p_l, cv_p_l = [], [], [], []
    k_s_l, v_s_l, lf_s_l, cv_s_l = [], [], [], []
    for i in range(DEPTH):
        g = norm_g[i]
        xp = half_ffn(xp, g[0], g[1], ffn_w_gate[i, 0], ffn_w_up[i, 0], ffn_w_down[i, 0])
        xs = half_ffn(xs, g[0], g[1], ffn_w_gate[i, 0], ffn_w_up[i, 0], ffn_w_down[i, 0])
        hp = rmsnorm(xp, g[2])
        hs = rmsnorm(xs, g[2])
        if i % N_MIXERS == 0:
            a = i // N_MIXERS
            qp, kp, vp, lfp = fox_project(hp, attn_w_in[a], attn_b_f[a])
            qs, kss, vs, lfs = fox_project(hs, attn_w_in[a], attn_b_f[a])
            op = fox_prompt(qp, kp, vp, lfp)
            osm = fox_sample(qs, kss, vs, lfs, cache_k, cache_v, cache_logf, page_table, a)
            mp = op.reshape(op.shape[0], op.shape[1], D_MODEL) @ attn_w_out[a]
            ms = osm.reshape(osm.shape[0], osm.shape[1], D_MODEL) @ attn_w_out[a]
            k_p_l.append(kp); v_p_l.append(vp); lf_p_l.append(lfp)
            k_s_l.append(kss); v_s_l.append(vs); lf_s_l.append(lfs)
        else:
            c = i // N_MIXERS
            bp, up = conv_project(hp, conv_w_in[c])
            bs, us = conv_project(hs, conv_w_in[c])
            up_ext = jnp.concatenate([jnp.zeros((n_prompt, CONV_W - 1, D_MODEL), up.dtype), up], axis=1)
            us_ext = jnp.concatenate([state_conv[c].astype(us.dtype), us], axis=1)
            mp = (bp * short_conv(up_ext, conv_kernel[c])) @ conv_w_out[c]
            ms = (bs * short_conv(us_ext, conv_kernel[c])) @ conv_w_out[c]
            cv_p_l.append(up_ext[:, -(CONV_W - 1):])
            cv_s_l.append(us_ext[:, -(CONV_W - 1):])
        xp = xp + rmsnorm(mp, g[3])
        xs = xs + rmsnorm(ms, g[3])
        xp = half_ffn(xp, g[4], g[5], ffn_w_gate[i, 1], ffn_w_up[i, 1], ffn_w_down[i, 1])
        xs = half_ffn(xs, g[4], g[5], ffn_w_gate[i, 1], ffn_w_up[i, 1], ffn_w_down[i, 1])
    return (xp[:, N_META:], xs,
            jnp.stack(k_p_l), jnp.stack(v_p_l), jnp.stack(lf_p_l), jnp.stack(cv_p_l),
            jnp.stack(k_s_l), jnp.stack(v_s_l), jnp.stack(lf_s_l), jnp.stack(cv_s_l))
```

```python
import functools

import jax
import jax.numpy as jnp
from jax import lax
from jax.experimental import pallas as pl
from jax.experimental.pallas import tpu as pltpu

D_MODEL = 1024
N_HEADS = 16
HEAD_DIM = 64
N_META = 16
CONV_W = 3
NORM_EPS = 1e-6
HALF = 0.5

LANES = 128
SUBLANES = 8
BLK = 128
PAD = (-N_META) % BLK
FF_CHUNK = 256
VMEM_LIMIT = 56 * 1024 * 1024
NEG = -1e30

F32 = jnp.float32
BF16 = jnp.bfloat16


def _dot(a, b):
    return jnp.dot(a, b, preferred_element_type=F32)


def _dot_nt(a, b):
    return lax.dot_general(a, b, (((1,), (1,)), ((), ())), preferred_element_type=F32)


def _div_pow2(x, n):
    assert n & (n - 1) == 0
    return lax.shift_right_logical(x, n.bit_length() - 1)


def _mod_pow2(x, n):
    assert n & (n - 1) == 0
    return x & (n - 1)


def _rms(x, g):
    ms = jnp.mean(x * x, axis=-1, keepdims=True)
    return x * lax.rsqrt(ms + NORM_EPS) * g


def _split3(x):
    hi = x.astype(BF16)
    r = x - hi.astype(F32)
    mid = r.astype(BF16)
    lo = (r - mid.astype(F32)).astype(BF16)
    return hi, mid, lo


def _const_spec(shape):
    return pl.BlockSpec(shape, lambda *_: (0,) * len(shape), pipeline_mode=pl.Buffered(1))


def _params(sem):
    return pltpu.CompilerParams(dimension_semantics=sem, vmem_limit_bytes=VMEM_LIMIT)


def _ffn_kernel(x_ref, gpre_ref, gpost_ref, wg_ref, wu_ref, wd_ref, o_ref, h_ref):
    x = x_ref[...]
    xn = _rms(x, gpre_ref[...]).astype(BF16)
    d_ff = wg_ref.shape[1]
    for c in range(d_ff // FF_CHUNK):
        sl = slice(c * FF_CHUNK, (c + 1) * FF_CHUNK)
        g = _dot(xn, wg_ref[:, sl])
        u = _dot(xn, wu_ref[:, sl])
        h_ref[:, sl] = (g * (1.0 / (1.0 + jnp.exp(-g))) * u).astype(BF16)
    y = _dot(h_ref[...], wd_ref[...])
    o_ref[...] = x + HALF * _rms(y, gpost_ref[...])


def _ffn(x, g_pre, g_post, wg, wu, wd, tm):
    rows, d = x.shape
    d_ff = wg.shape[1]
    row_spec = pl.BlockSpec((tm, d), lambda i: (i, 0))
    return pl.pallas_call(
        _ffn_kernel,
        grid=(rows // tm,),
        in_specs=[row_spec, _const_spec((1, d)), _const_spec((1, d)),
                  _const_spec((d, d_ff)), _const_spec((d, d_ff)), _const_spec((d_ff, d))],
        out_specs=row_spec,
        out_shape=jax.ShapeDtypeStruct((rows, d), F32),
        scratch_shapes=[pltpu.VMEM((tm, d_ff), BF16)],
        compiler_params=_params(("parallel",)),
        name="half_ffn",
    )(x, g_pre, g_post, wg, wu, wd)


def _log_sigmoid(z):
    return jnp.minimum(z, 0.0) - jnp.log1p(jnp.exp(-jnp.abs(z)))


def _attn_proj_core(x_ref, g_ref, wq_ref, wk_ref, wv_ref, wf_ref, bf_ref):
    hn = _rms(x_ref[...], g_ref[...]).astype(BF16)
    q = _dot(hn, wq_ref[...])
    k = _dot(hn, wk_ref[...])
    v = _dot(hn, wv_ref[...])
    lf = _log_sigmoid(_dot(hn, wf_ref[...]) + bf_ref[...])
    return q, k, v, lf


def _attn_proj_sample_kernel(x_ref, g_ref, wq_ref, wk_ref, wv_ref, wf_ref, bf_ref,
                             q_ref, k_ref, v_ref, lf_ref):
    q, k, v, lf = _attn_proj_core(x_ref, g_ref, wq_ref, wk_ref, wv_ref, wf_ref, bf_ref)
    q_ref[...] = q
    k_ref[...] = k
    v_ref[...] = v
    lf_ref[...] = lf[:, :N_HEADS]


def _attn_proj_prompt_kernel(x_ref, g_ref, wq_ref, wk_ref, wv_ref, wf_ref, bf_ref,
                             k_ref, v_ref, lf_ref, qa_ref, ka_ref, vb_ref, cb_ref,
                             carry_ref, *, blocks_per_batch):
    i = pl.program_id(0)
    tm = x_ref.shape[0]
    q, k, v, lf = _attn_proj_core(x_ref, g_ref, wq_ref, wk_ref, wv_ref, wf_ref, bf_ref)
    k_ref[...] = k
    v_ref[...] = v
    vb_ref[...] = v.astype(BF16)
    lf_ref[...] = lf[:, :N_HEADS]

    lane = lax.broadcasted_iota(jnp.int32, (BLK, LANES), 1)
    lf = jnp.where(lax.broadcasted_iota(jnp.int32, lf.shape, 1) < N_HEADS, lf, 0.0)
    tril = (lax.broadcasted_iota(jnp.int32, (BLK, BLK), 0)
            >= lax.broadcasted_iota(jnp.int32, (BLK, BLK), 1)).astype(BF16)
    scale = HEAD_DIM ** -0.5
    nblk = tm // BLK

    @pl.when(i == 0)
    def _():
        carry_ref[...] = jnp.zeros_like(carry_ref)

    cb_ref[...] = jnp.zeros_like(cb_ref)
    for jb in range(nblk):
        rows = slice(jb * BLK, (jb + 1) * BLK)
        first = (i * nblk + jb) % blocks_per_batch == 0
        base = jnp.where(first, 0.0, carry_ref[0:1, :])
        hi, mid, lo = _split3(lf[rows])
        floc = _dot(tril, hi) + _dot(tril, mid) + _dot(tril, lo)
        cb_ref[0, jb:jb + 1, :] = base
        carry_ref[0:1, :] = base + floc[BLK - 1:BLK, :]

        f_hi = floc.astype(BF16).astype(F32)
        f_lo = floc - f_hi
        fq_odd = f_hi + pltpu.roll(f_lo, N_HEADS, axis=1)
        fq_even = pltpu.roll(fq_odd, 64, axis=1)
        fk_odd = pltpu.roll(-fq_odd, 32, axis=1)
        fk_even = pltpu.roll(fk_odd, 64, axis=1)
        for hp in range(N_HEADS // 2):
            cols = slice(hp * LANES, (hp + 1) * LANES)
            qc = q[rows, cols] * scale
            kc = k[rows, cols]
            he, ho = 2 * hp, 2 * hp + 1
            one_q_even = (lane == 96 + he) | (lane == 112 + he)
            one_q_odd = (lane == 32 + ho) | (lane == 48 + ho)
            one_k_even = (lane == 64 + he) | (lane == 80 + he)
            one_k_odd = (lane == ho) | (lane == 16 + ho)
            low = lane < 64
            qa_e = jnp.where(low, qc, jnp.where(one_q_even, 1.0, fq_even))
            qa_o = jnp.where(low, jnp.where(one_q_odd, 1.0, fq_odd), qc)
            ka_e = jnp.where(low, kc, jnp.where(one_k_even, 1.0, fk_even))
            ka_o = jnp.where(low, jnp.where(one_k_odd, 1.0, fk_odd), kc)
            qa_ref[rows, he * LANES:(he + 1) * LANES] = qa_e.astype(BF16)
            qa_ref[rows, ho * LANES:(ho + 1) * LANES] = qa_o.astype(BF16)
            ka_ref[rows, he * LANES:(he + 1) * LANES] = ka_e.astype(BF16)
            ka_ref[rows, ho * LANES:(ho + 1) * LANES] = ka_o.astype(BF16)


def _attn_proj(x, g, wq, wk, wv, wf, bf, tm, blocks_per_batch=None):
    rows, d = x.shape
    row_spec = pl.BlockSpec((tm, d), lambda i: (i, 0))
    lf_spec = pl.BlockSpec((tm, N_HEADS), lambda i: (i, 0))
    in_specs = [row_spec, _const_spec((1, d)), _const_spec((d, d)), _const_spec((d, d)),
                _const_spec((d, d)), _const_spec((d, LANES)), _const_spec((1, LANES))]
    f32_rows = jax.ShapeDtypeStruct((rows, d), F32)
    lf_shape = jax.ShapeDtypeStruct((rows, N_HEADS), F32)
    if blocks_per_batch is None:
        return pl.pallas_call(
            _attn_proj_sample_kernel,
            grid=(rows // tm,),
            in_specs=in_specs,
            out_specs=[row_spec, row_spec, row_spec, lf_spec],
            out_shape=[f32_rows, f32_rows, f32_rows, lf_shape],
            compiler_params=_params(("parallel",)),
            name="attn_proj_sample",
        )(x, g, wq, wk, wv, wf, bf)
    aug_spec = pl.BlockSpec((tm, 2 * d), lambda i: (i, 0))
    n_tiles = rows // tm
    return pl.pallas_call(
        functools.partial(_attn_proj_prompt_kernel, blocks_per_batch=blocks_per_batch),
        grid=(n_tiles,),
        in_specs=in_specs,
        out_specs=[row_spec, row_spec, lf_spec, aug_spec, aug_spec, row_spec,
                   pl.BlockSpec((1, SUBLANES, LANES), lambda i: (i, 0, 0))],
        out_shape=[f32_rows, f32_rows, lf_shape,
                   jax.ShapeDtypeStruct((rows, 2 * d), BF16),
                   jax.ShapeDtypeStruct((rows, 2 * d), BF16),
                   jax.ShapeDtypeStruct((rows, d), BF16),
                   jax.ShapeDtypeStruct((n_tiles, SUBLANES, LANES), F32)],
        scratch_shapes=[pltpu.VMEM((SUBLANES, LANES), F32)],
        compiler_params=_params(("arbitrary",)),
        name="attn_proj_prompt",
    )(x, g, wq, wk, wv, wf, bf)


def _prompt_attn_kernel(cb_ref, qa_ref, ka_ref, vb_ref, o_ref):
    b = pl.program_id(0)
    hp = pl.program_id(1)
    i = pl.program_id(2)
    nb = pl.num_programs(2)
    q = qa_ref[...]
    r0 = (b * N_HEADS + 2 * hp) * nb
    r1 = r0 + nb
    ci0 = cb_ref[r0 + i]
    ci1 = cb_ref[r1 + i]
    lane = lax.broadcasted_iota(jnp.int32, (BLK, BLK), 1)
    row = lax.broadcasted_iota(jnp.int32, (BLK, BLK), 0)
    low = lane < HEAD_DIM
    zeros_k = jnp.zeros((BLK, LANES), BF16)

    def step(j, carry):
        m0, l0, m1, l1, acc = carry
        start = pl.multiple_of(j * BLK, BLK)
        kj = ka_ref[pl.ds(start, BLK), :]
        vj = vb_ref[pl.ds(start, BLK), :]
        kbd = jnp.concatenate(
            [jnp.concatenate([kj[:, :LANES], zeros_k], axis=1),
             jnp.concatenate([zeros_k, kj[:, LANES:]], axis=1)], axis=0)
        s = _dot_nt(q, kbd)
        key = j * BLK + lane
        valid = (key <= i * BLK + row) & (key >= PAD)
        s0 = jnp.where(valid, s[:, :BLK], NEG)
        s1 = jnp.where(valid, s[:, BLK:], NEG)
        d0 = ci0 - cb_ref[r0 + j]
        d1 = ci1 - cb_ref[r1 + j]
        mn0 = jnp.maximum(m0, jnp.max(s0, axis=-1, keepdims=True) + d0)
        mn1 = jnp.maximum(m1, jnp.max(s1, axis=-1, keepdims=True) + d1)
        p0 = jnp.exp(s0 - (mn0 - d0))
        p1 = jnp.exp(s1 - (mn1 - d1))
        a0 = jnp.exp(m0 - mn0)
        a1 = jnp.exp(m1 - mn1)
        l0 = a0 * l0 + jnp.sum(p0, axis=-1, keepdims=True)
        l1 = a1 * l1 + jnp.sum(p1, axis=-1, keepdims=True)
        pcat = jnp.concatenate([p0, p1], axis=1).astype(BF16)
        vf = vj.astype(F32)
        vbd = jnp.concatenate([jnp.where(low, vf, 0.0), jnp.where(low, 0.0, vf)], axis=0).astype(BF16)
        acc = jnp.where(low, a0, a1) * acc + _dot(pcat, vbd)
        return mn0, l0, mn1, l1, acc

    col = jnp.full((BLK, 1), NEG, F32)
    zcol = jnp.zeros((BLK, 1), F32)
    _, l0, _, l1, acc = lax.fori_loop(0, i + 1, step, (col, zcol, col, zcol, jnp.zeros((BLK, LANES), F32)))
    o_ref[...] = (acc * jnp.where(low, 1.0 / l0, 1.0 / l1)).astype(BF16)


def _prompt_attn(cb, qa, ka, vb, n_batch, lp):
    nb = lp // BLK
    d = vb.shape[1]
    ka3 = ka.reshape(n_batch, lp, 2 * d)
    vb3 = vb.reshape(n_batch, lp, d)
    return pl.pallas_call(
        _prompt_attn_kernel,
        grid=(n_batch, N_HEADS // 2, nb),
        in_specs=[pl.BlockSpec(memory_space=pltpu.SMEM),
                  pl.BlockSpec((BLK, 2 * LANES), lambda b, hp, i: (b * nb + i, hp)),
                  pl.BlockSpec((None, lp, 2 * LANES), lambda b, hp, i: (b, 0, hp)),
                  pl.BlockSpec((None, lp, LANES), lambda b, hp, i: (b, 0, hp))],
        out_specs=pl.BlockSpec((BLK, LANES), lambda b, hp, i: (b * nb + i, hp)),
        out_shape=jax.ShapeDtypeStruct((n_batch * lp, d), BF16),
        compiler_params=_params(("parallel", "parallel", "parallel")),
        name="prompt_attn",
    )(cb, qa, ka3, vb3)


def _decode_attn_kernel(pt_ref, q_ref, kn_ref, vn_ref, lfn_ref, *rest, n_pages_step, n_new):
    np_ = n_pages_step
    k_refs = rest[0:np_]
    v_refs = rest[np_:2 * np_]
    lf_refs = rest[2 * np_:3 * np_]
    o_ref = rest[3 * np_]
    qbd_ref, fc_ref, ct_ref, m_ref, l_ref, acc_ref, kpad_ref, vpad_ref = rest[3 * np_ + 1:]
    s_idx = pl.program_id(1)
    rows = n_new * N_HEADS
    d = q_ref.shape[-1]
    lane = lax.broadcasted_iota(jnp.int32, (rows, LANES), 1)
    row = lax.broadcasted_iota(jnp.int32, (rows, LANES), 0)
    t_of_row = _div_pow2(row, N_HEADS)

    def tile_rows(x):
        return jnp.concatenate([x] * n_new, axis=0)

    def own_head():
        head_of_lane = _div_pow2(lax.broadcasted_iota(jnp.int32, (rows, d), 1), HEAD_DIM)
        head_of_row = _mod_pow2(lax.broadcasted_iota(jnp.int32, (rows, d), 0), N_HEADS)
        return head_of_lane == head_of_row

    @pl.when(s_idx == 0)
    def _():
        q = q_ref[...] * (HEAD_DIM ** -0.5)
        qb = jnp.concatenate([jnp.broadcast_to(q[t:t + 1, :], (N_HEADS, d)) for t in range(n_new)], axis=0)
        qbd = jnp.where(own_head(), qb, 0.0).astype(BF16)
        qbd_ref[...] = qbd
        kpad_ref[...] = jnp.zeros_like(kpad_ref)
        vpad_ref[...] = jnp.zeros_like(vpad_ref)
        kpad_ref[0:n_new, :] = kn_ref[...]
        vpad_ref[0:n_new, :] = vn_ref[...]
        lfn = lfn_ref[...]
        lane_h = lax.broadcasted_iota(jnp.int32, (N_HEADS, LANES), 1)
        run = jnp.zeros((N_HEADS, 1), F32)
        fnt = jnp.zeros((N_HEADS, LANES), F32)
        cols = []
        for t in range(n_new):
            run = run + lfn[:, t:t + 1]
            cols.append(run)
            fnt = jnp.where(lane_h == t, run, fnt)
        fn_col = jnp.concatenate(cols, axis=0)
        fc_ref[...] = fn_col
        s = _dot_nt(qbd, kpad_ref[...].astype(BF16))
        s = s + fn_col - tile_rows(fnt)
        s = jnp.where(lane <= t_of_row, s, NEG)
        m = jnp.max(s, axis=-1, keepdims=True)
        p = jnp.exp(s - m)
        m_ref[...] = m
        l_ref[...] = jnp.sum(p, axis=-1, keepdims=True)
        acc_ref[...] = _dot(p.astype(BF16), vpad_ref[...].astype(BF16))
        ct_ref[...] = jnp.zeros_like(ct_ref)

    qbd = qbd_ref[...]
    strict_lower = (lax.broadcasted_iota(jnp.int32, (LANES, LANES), 0)
                    > lax.broadcasted_iota(jnp.int32, (LANES, LANES), 1)).astype(BF16)
    ct = ct_ref[...]
    fn_col = fc_ref[...]
    scores = []
    for p_i in range(np_):
        lft = lf_refs[p_i][...]
        hi, mid, lo = _split3(lft)
        gloc = _dot(hi, strict_lower) + _dot(mid, strict_lower) + _dot(lo, strict_lower)
        s = _dot_nt(qbd, k_refs[p_i][...].astype(BF16))
        scores.append(s + tile_rows(gloc) + (fn_col + tile_rows(ct)))
        ct = ct + jnp.sum(lft, axis=-1, keepdims=True)
    ct_ref[...] = ct
    m_old = m_ref[...]
    m_new = m_old
    for s in scores:
        m_new = jnp.maximum(m_new, jnp.max(s, axis=-1, keepdims=True))
    alpha = jnp.exp(m_old - m_new)
    l_new = alpha * l_ref[...]
    acc = alpha * acc_ref[...]
    for p_i in range(np_):
        p = jnp.exp(scores[p_i] - m_new)
        l_new = l_new + jnp.sum(p, axis=-1, keepdims=True)
        acc = acc + _dot(p.astype(BF16), v_refs[p_i][...].astype(BF16))
    m_ref[...] = m_new
    l_ref[...] = l_new
    acc_ref[...] = acc

    @pl.when(s_idx == pl.num_programs(1) - 1)
    def _():
        o = jnp.where(own_head(), acc * (1.0 / l_new), 0.0)
        for t in range(n_new):
            o_ref[t:t + 1, :] = jnp.sum(o[t * N_HEADS:(t + 1) * N_HEADS, :], axis=0, keepdims=True)


def _decode_attn(page_table, q, kn, vn, lfn_t, cache_k, cache_v, cache_lf_t, n_pages_step):
    db, n_new, d = q.shape
    n_pages = page_table.shape[1]
    page = cache_k.shape[1]
    np_ = n_pages_step
    steps = n_pages // np_
    rows = n_new * N_HEADS

    def page_map(p_i):
        return lambda b, s, pt: (pt[b, n_pages - 1 - (s * np_ + p_i)], 0, 0)

    new_spec = pl.BlockSpec((None, n_new, d), lambda b, s, pt: (b, 0, 0))
    in_specs = ([new_spec, new_spec, new_spec,
                 pl.BlockSpec((None, N_HEADS, LANES), lambda b, s, pt: (b, 0, 0))]
                + [pl.BlockSpec((None, page, d), page_map(p_i)) for p_i in range(np_)]
                + [pl.BlockSpec((None, page, d), page_map(p_i)) for p_i in range(np_)]
                + [pl.BlockSpec((None, N_HEADS, page), page_map(p_i)) for p_i in range(np_)])
    grid_spec = pltpu.PrefetchScalarGridSpec(
        num_scalar_prefetch=1,
        grid=(db, steps),
        in_specs=in_specs,
        out_specs=pl.BlockSpec((None, n_new, d), lambda b, s, pt: (b, 0, 0)),
        scratch_shapes=[pltpu.VMEM((rows, d), BF16),
                        pltpu.VMEM((rows, 1), F32),
                        pltpu.VMEM((N_HEADS, 1), F32),
                        pltpu.VMEM((rows, 1), F32),
                        pltpu.VMEM((rows, 1), F32),
                        pltpu.VMEM((rows, d), F32),
                        pltpu.VMEM((LANES, d), F32),
                        pltpu.VMEM((LANES, d), F32)])
    return pl.pallas_call(
        functools.partial(_decode_attn_kernel, n_pages_step=np_, n_new=n_new),
        grid_spec=grid_spec,
        out_shape=jax.ShapeDtypeStruct((db, n_new, d), F32),
        compiler_params=_params(("parallel", "arbitrary")),
        name="decode_attn",
    )(page_table, q, kn, vn, lfn_t, *([cache_k] * np_), *([cache_v] * np_), *([cache_lf_t] * np_))


def _out_proj_kernel(x_ref, o_ref, w_ref, g_ref, y_ref):
    m = _dot(o_ref[...].astype(BF16), w_ref[...])
    y_ref[...] = x_ref[...] + _rms(m, g_ref[...])


def _out_proj(x, o, w, g, tm):
    rows, d = x.shape
    row_spec = pl.BlockSpec((tm, d), lambda i: (i, 0))
    return pl.pallas_call(
        _out_proj_kernel,
        grid=(rows // tm,),
        in_specs=[row_spec, row_spec, _const_spec((d, d)), _const_spec((1, d))],
        out_specs=row_spec,
        out_shape=jax.ShapeDtypeStruct((rows, d), F32),
        compiler_params=_params(("parallel",)),
        name="mixer_out_proj",
    )(x, o, w, g)


def _conv_core(x_ref, gpre_ref, wb_ref, wc_ref, wh_ref):
    hn = _rms(x_ref[...], gpre_ref[...]).astype(BF16)
    return _dot(hn, wb_ref[...]), _dot(hn, wc_ref[...]) * _dot(hn, wh_ref[...])


def _conv_finish(x_ref, gate, conv, wo_ref, gpost_ref, y_ref):
    m = _dot((gate * conv).astype(BF16), wo_ref[...])
    y_ref[...] = x_ref[...] + _rms(m, gpost_ref[...])


def _conv_prompt_kernel(x_ref, gpre_ref, wb_ref, wc_ref, wh_ref, ck_ref, wo_ref, gpost_ref,
                        y_ref, tail_ref, u_ref, *, blocks_per_batch):
    i = pl.program_id(0)
    tm = x_ref.shape[0]
    nblk = tm // BLK
    hist = SUBLANES

    @pl.when(i == 0)
    def _():
        u_ref[0:hist, :] = jnp.zeros((hist, u_ref.shape[1]), F32)

    @pl.when(i > 0)
    def _():
        u_ref[0:hist, :] = u_ref[tm:tm + hist, :]

    gate, u = _conv_core(x_ref, gpre_ref, wb_ref, wc_ref, wh_ref)
    row = lax.broadcasted_iota(jnp.int32, (BLK, 1), 0)
    for jb in range(nblk):
        first = (i * nblk + jb) % blocks_per_batch == 0
        n_pad = jnp.where(first, PAD, 0)
        ub = jnp.where(row < n_pad, 0.0, u[jb * BLK:(jb + 1) * BLK])
        u_ref[hist + jb * BLK:hist + (jb + 1) * BLK, :] = ub
        tail_ref[0, jb * hist:(jb + 1) * hist, :] = ub[BLK - hist:, :]
    ck = ck_ref[...]
    conv = (ck[0:1, :] * u_ref[hist - 2:hist - 2 + tm, :]
            + ck[1:2, :] * u_ref[hist - 1:hist - 1 + tm, :]
            + ck[2:3, :] * u_ref[hist:hist + tm, :])
    _conv_finish(x_ref, gate, conv, wo_ref, gpost_ref, y_ref)


def _conv_sample_kernel(x_ref, gpre_ref, wb_ref, wc_ref, wh_ref, ck_ref, wo_ref, gpost_ref,
                        h1_ref, h2_ref, y_ref, uo_ref, u_ref, *, n_new):
    tm = x_ref.shape[0]
    hist = SUBLANES
    gate, u = _conv_core(x_ref, gpre_ref, wb_ref, wc_ref, wh_ref)
    uo_ref[...] = u
    u_ref[0:hist, :] = jnp.zeros((hist, u_ref.shape[1]), F32)
    u_ref[hist:hist + tm, :] = u
    t = _mod_pow2(lax.broadcasted_iota(jnp.int32, (tm, 1), 0), n_new)
    ck = ck_ref[...]
    conv = (ck[0:1, :] * jnp.where(t >= 2, u_ref[hist - 2:hist - 2 + tm, :], h2_ref[...])
            + ck[1:2, :] * jnp.where(t >= 1, u_ref[hist - 1:hist - 1 + tm, :], h1_ref[...])
            + ck[2:3, :] * u)
    _conv_finish(x_ref, gate, conv, wo_ref, gpost_ref, y_ref)


def _conv_mixer(x, g_pre, wb, wc, wh, ck, wo, g_post, tm, blocks_per_batch=None, hist1=None, hist2=None,
                n_new=None):
    rows, d = x.shape
    row_spec = pl.BlockSpec((tm, d), lambda i: (i, 0))
    w_spec = _const_spec((d, d))
    in_specs = [row_spec, _const_spec((1, d)), w_spec, w_spec, w_spec, _const_spec((CONV_W, d)),
                w_spec, _const_spec((1, d))]
    f32_rows = jax.ShapeDtypeStruct((rows, d), F32)
    scratch = [pltpu.VMEM((tm + 2 * SUBLANES, d), F32)]
    if blocks_per_batch is None:
        return pl.pallas_call(
            functools.partial(_conv_sample_kernel, n_new=n_new),
            grid=(rows // tm,),
            in_specs=in_specs + [row_spec, row_spec],
            out_specs=[row_spec, row_spec],
            out_shape=[f32_rows, f32_rows],
            scratch_shapes=scratch,
            compiler_params=_params(("parallel",)),
            name="conv_mixer_sample",
        )(x, g_pre, wb, wc, wh, ck, wo, g_post, hist1, hist2)
    n_tiles = rows // tm
    tail_rows = tm // BLK * SUBLANES
    return pl.pallas_call(
        functools.partial(_conv_prompt_kernel, blocks_per_batch=blocks_per_batch),
        grid=(n_tiles,),
        in_specs=in_specs,
        out_specs=[row_spec, pl.BlockSpec((1, tail_rows, d), lambda i: (i, 0, 0))],
        out_shape=[f32_rows, jax.ShapeDtypeStruct((n_tiles, tail_rows, d), F32)],
        scratch_shapes=scratch,
        compiler_params=_params(("arbitrary",)),
        name="conv_mixer_prompt",
    )(x, g_pre, wb, wc, wh, ck, wo, g_post)


def kernel(x_prompt, x_sample, cache_k, cache_v, cache_logf, state_conv, page_table, meta_tokens, norm_g,
           ffn_w_gate, ffn_w_up, ffn_w_down, attn_w_in, attn_b_f, attn_w_out, conv_w_in, conv_kernel,
           conv_w_out):
    n_batch, seq, d = x_prompt.shape
    db, n_new, _ = x_sample.shape
    depth = norm_g.shape[0]
    n_pool, page = cache_k.shape[1], cache_k.shape[2]
    lp = PAD + N_META + seq
    assert d == D_MODEL and lp % BLK == 0 and page == LANES
    bpb = lp // BLK
    tm_p = 4 * BLK
    tm_s = db * n_new
    assert (n_batch * lp) % tm_p == 0 and tm_s % SUBLANES == 0

    meta = jnp.broadcast_to(meta_tokens[None].astype(F32), (n_batch, N_META, d))
    xp = jnp.concatenate([jnp.zeros((n_batch, PAD, d), F32), meta, x_prompt], axis=1).reshape(n_batch * lp, d)
    xs = x_sample.reshape(tm_s, d)

    def row(v):
        return v.reshape(1, -1).astype(F32)

    k_p, v_p, lf_p, cv_p, k_s, v_s, lf_s, cv_s = [], [], [], [], [], [], [], []
    for i in range(depth):
        g = norm_g[i]
        wg, wu, wd = (w[i].astype(BF16) for w in (ffn_w_gate, ffn_w_up, ffn_w_down))
        xp = _ffn(xp, row(g[0]), row(g[1]), wg[0], wu[0], wd[0], tm_p)
        xs = _ffn(xs, row(g[0]), row(g[1]), wg[0], wu[0], wd[0], tm_s)
        if i % 2 == 0:
            a = i // 2
            w_in = attn_w_in[a]
            wq, wk, wv = (w_in[:, j * d:(j + 1) * d].astype(BF16) for j in range(3))
            wf = jnp.pad(w_in[:, 3 * d:], ((0, 0), (0, LANES - N_HEADS))).astype(BF16)
            bf = jnp.pad(attn_b_f[a], (0, LANES - N_HEADS)).reshape(1, LANES).astype(F32)
            wo = attn_w_out[a].astype(BF16)

            kp, vp, lfp, qa, ka, vb, cb = _attn_proj(xp, row(g[2]), wq, wk, wv, wf, bf, tm_p, bpb)
            cb = cb[:, :tm_p // BLK, :N_HEADS].reshape(n_batch, bpb, N_HEADS)
            cb = cb.transpose(0, 2, 1).reshape(n_batch * N_HEADS * bpb)
            op = _prompt_attn(cb, qa, ka, vb, n_batch, lp)
            xp = _out_proj(xp, op, wo, row(g[3]), tm_p)

            qs, ks, vs, lfs = _attn_proj(xs, row(g[2]), wq, wk, wv, wf, bf, tm_s)
            lfn_t = jnp.pad(lfs.reshape(db, n_new, N_HEADS).transpose(0, 2, 1),
                            ((0, 0), (0, 0), (0, LANES - n_new)))
            ck_pages = cache_k[a].reshape(n_pool, page, d)
            cv_pages = cache_v[a].reshape(n_pool, page, d)
            clf_t = cache_logf[a].astype(F32).transpose(0, 2, 1)
            osm = _decode_attn(page_table, qs.reshape(db, n_new, d), ks.reshape(db, n_new, d),
                               vs.reshape(db, n_new, d), lfn_t, ck_pages, cv_pages, clf_t, 4)
            xs = _out_proj(xs, osm.reshape(tm_s, d), wo, row(g[3]), tm_s)

            k_p.append(kp.reshape(n_batch, lp, N_HEADS, HEAD_DIM)[:, PAD:])
            v_p.append(vp.reshape(n_batch, lp, N_HEADS, HEAD_DIM)[:, PAD:])
            lf_p.append(lfp.reshape(n_batch, lp, N_HEADS)[:, PAD:])
            k_s.append(ks.reshape(db, n_new, N_HEADS, HEAD_DIM))
            v_s.append(vs.reshape(db, n_new, N_HEADS, HEAD_DIM))
            lf_s.append(lfs.reshape(db, n_new, N_HEADS))
        else:
            c = i // 2
            w_in = conv_w_in[c]
            wb, wc, wh = (w_in[:, j * d:(j + 1) * d].astype(BF16) for j in range(3))
            wo = conv_w_out[c].astype(BF16)
            ck = conv_kernel[c].astype(F32)
            xp, tails = _conv_mixer(xp, row(g[2]), wb, wc, wh, ck, wo, row(g[3]), tm_p, blocks_per_batch=bpb)
            tails = tails.reshape(n_batch, bpb, SUBLANES, d)
            cv_p.append(tails[:, bpb - 1, SUBLANES - (CONV_W - 1):])

            st = state_conv[c].astype(F32)
            zero = jnp.zeros((db, n_new, d), F32)
            hist1 = zero.at[:, 0].set(st[:, 1]).reshape(tm_s, d)
            hist2 = zero.at[:, 0].set(st[:, 0]).at[:, 1].set(st[:, 1]).reshape(tm_s, d)
            xs, us = _conv_mixer(xs, row(g[2]), wb, wc, wh, ck, wo, row(g[3]), tm_s, hist1=hist1,
                                 hist2=hist2, n_new=n_new)
            us_ext = jnp.concatenate([st, us.reshape(db, n_new, d)], axis=1)
            cv_s.append(us_ext[:, -(CONV_W - 1):])
        xp = _ffn(xp, row(g[4]), row(g[5]), wg[1], wu[1], wd[1], tm_p)
        xs = _ffn(xs, row(g[4]), row(g[5]), wg[1], wu[1], wd[1], tm_s)

    y_prompt = xp.reshape(n_batch, lp, d)[:, PAD + N_META:]
    y_sample = xs.reshape(db, n_new, d)
    return (y_prompt, y_sample, jnp.stack(k_p), jnp.stack(v_p), jnp.stack(lf_p), jnp.stack(cv_p),
            jnp.stack(k_s), jnp.stack(v_s), jnp.stack(lf_s), jnp.stack(cv_s))
```

```python
import functools

import jax
import jax.numpy as jnp
from jax import lax
from jax.experimental import pallas as pl
from jax.experimental.pallas import tpu as pltpu

D_MODEL = 1024
N_HEADS = 16
HEAD_DIM = 64
N_META = 16
CONV_W = 3
NORM_EPS = 1e-6
HALF = 0.5

LANES = 128
SUBLANES = 8
BLK = 128
CHUNK = 2 * BLK
ROW_TILE = 2 * CHUNK
V_ROWS = HEAD_DIM + 16
HEADS_PER_STEP = 8
FF_CHUNK = 256
VMEM_LIMIT = 56 * 1024 * 1024
NEG = -1e30
LOG2E = 1.4426950408889634

F32 = jnp.float32
BF16 = jnp.bfloat16


def _dot(a, b):
    return jnp.dot(a, b, preferred_element_type=F32)


def _dot_nt(a, b):
    return lax.dot_general(a, b, (((1,), (1,)), ((), ())), preferred_element_type=F32)


def _div_pow2(x, n):
    assert n & (n - 1) == 0
    return lax.shift_right_logical(x, n.bit_length() - 1)


def _mod_pow2(x, n):
    assert n & (n - 1) == 0
    return x & (n - 1)


def _rms(x, g):
    ms = jnp.mean(x * x, axis=-1, keepdims=True)
    return x * lax.rsqrt(ms + NORM_EPS) * g


def _split3(x):
    hi = x.astype(BF16)
    r = x - hi.astype(F32)
    mid = r.astype(BF16)
    lo = (r - mid.astype(F32)).astype(BF16)
    return hi, mid, lo


def _const_spec(shape):
    return pl.BlockSpec(shape, lambda *_: (0,) * len(shape), pipeline_mode=pl.Buffered(1))


def _params(sem):
    return pltpu.CompilerParams(dimension_semantics=sem, vmem_limit_bytes=VMEM_LIMIT)


def _ffn_kernel(x_ref, gpre_ref, gpost_ref, wg_ref, wu_ref, wd_ref, o_ref, h_ref):
    x = x_ref[...]
    xn = _rms(x, gpre_ref[...]).astype(BF16)
    d_ff = wg_ref.shape[1]
    for c in range(d_ff // FF_CHUNK):
        sl = slice(c * FF_CHUNK, (c + 1) * FF_CHUNK)
        g = _dot(xn, wg_ref[:, sl])
        u = _dot(xn, wu_ref[:, sl])
        h_ref[:, sl] = (g * (1.0 / (1.0 + jnp.exp(-g))) * u).astype(BF16)
    y = _dot(h_ref[...], wd_ref[...])
    o_ref[...] = x + HALF * _rms(y, gpost_ref[...])


def _ffn(x, g_pre, g_post, wg, wu, wd, tm):
    rows, d = x.shape
    d_ff = wg.shape[1]
    assert rows % tm == 0 and d_ff % FF_CHUNK == 0
    row_spec = pl.BlockSpec((tm, d), lambda i: (i, 0))
    return pl.pallas_call(
        _ffn_kernel,
        grid=(rows // tm,),
        in_specs=[row_spec, _const_spec((1, d)), _const_spec((1, d)),
                  _const_spec((d, d_ff)), _const_spec((d, d_ff)), _const_spec((d_ff, d))],
        out_specs=row_spec,
        out_shape=jax.ShapeDtypeStruct((rows, d), F32),
        scratch_shapes=[pltpu.VMEM((tm, d_ff), BF16)],
        compiler_params=_params(("parallel",)),
        name="half_ffn",
    )(x, g_pre, g_post, wg, wu, wd)


def _log_sigmoid(z):
    return jnp.minimum(z, 0.0) - jnp.log1p(jnp.exp(-jnp.abs(z)))


def _attn_proj_core(x_ref, g_ref, wq_ref, wk_ref, wv_ref, wf_ref, bf_ref):
    hn = _rms(x_ref[...], g_ref[...]).astype(BF16)
    q = _dot(hn, wq_ref[...])
    k = _dot(hn, wk_ref[...])
    v = _dot(hn, wv_ref[...])
    lf = _log_sigmoid(_dot(hn, wf_ref[...]) + bf_ref[...])
    return q, k, v, lf


def _attn_proj_sample_kernel(x_ref, g_ref, wq_ref, wk_ref, wv_ref, wf_ref, bf_ref,
                             q_ref, k_ref, v_ref, lf_ref):
    q, k, v, lf = _attn_proj_core(x_ref, g_ref, wq_ref, wk_ref, wv_ref, wf_ref, bf_ref)
    q_ref[...] = q
    k_ref[...] = k
    v_ref[...] = v
    lf_ref[...] = lf[:, :N_HEADS]


def _attn_proj_prompt_kernel(x_ref, g_ref, wq_ref, wk_ref, wv_ref, wf_ref, bf_ref,
                             kt_ref, vt32_ref, lf_ref, qa_ref, ka_ref, vt_ref, cb_ref,
                             carry_ref, *, tiles_per_batch):
    i = pl.program_id(0)
    tm = x_ref.shape[0]
    q, k, v, lf = _attn_proj_core(x_ref, g_ref, wq_ref, wk_ref, wv_ref, wf_ref, bf_ref)
    kt_ref[...] = k.T
    vt = v.T
    vt32_ref[...] = vt
    vt_ref[...] = vt.astype(BF16)
    lf_ref[...] = lf[:, :N_HEADS]

    lane = lax.broadcasted_iota(jnp.int32, (BLK, LANES), 1)
    lf = jnp.where(lax.broadcasted_iota(jnp.int32, lf.shape, 1) < N_HEADS, lf * LOG2E, 0.0)
    tril = (lax.broadcasted_iota(jnp.int32, (BLK, BLK), 0)
            >= lax.broadcasted_iota(jnp.int32, (BLK, BLK), 1)).astype(BF16)
    scale = HEAD_DIM ** -0.5 * LOG2E

    first = i % tiles_per_batch == 0
    base = jnp.where(first, 0.0, carry_ref[0:1, :])
    cb_ref[...] = jnp.zeros_like(cb_ref)
    cb_ref[0, 0:1, :] = base
    off = jnp.zeros((1, LANES), F32)
    for jb in range(tm // BLK):
        rows = slice(jb * BLK, (jb + 1) * BLK)
        hi, mid, lo = _split3(lf[rows])
        floc = _dot(tril, hi) + _dot(tril, mid) + _dot(tril, lo) + off
        off = floc[BLK - 1:BLK, :]

        f_hi = floc.astype(BF16).astype(F32)
        f_lo = floc - f_hi
        fq_odd = f_hi + pltpu.roll(f_lo, N_HEADS, axis=1)
        fq_even = pltpu.roll(fq_odd, 64, axis=1)
        fk_odd = pltpu.roll(-fq_odd, 32, axis=1)
        fk_even = pltpu.roll(fk_odd, 64, axis=1)
        for hp in range(N_HEADS // 2):
            cols = slice(hp * LANES, (hp + 1) * LANES)
            qc = q[rows, cols] * scale
            kc = k[rows, cols]
            he, ho = 2 * hp, 2 * hp + 1
            one_q_even = (lane == 96 + he) | (lane == 112 + he)
            one_q_odd = (lane == 32 + ho) | (lane == 48 + ho)
            one_k_even = (lane == 64 + he) | (lane == 80 + he)
            one_k_odd = (lane == ho) | (lane == 16 + ho)
            low = lane < 64
            qa_e = jnp.where(low, qc, jnp.where(one_q_even, 1.0, fq_even))
            qa_o = jnp.where(low, jnp.where(one_q_odd, 1.0, fq_odd), qc)
            ka_e = jnp.where(low, kc, jnp.where(one_k_even, 1.0, fk_even))
            ka_o = jnp.where(low, jnp.where(one_k_odd, 1.0, fk_odd), kc)
            qa_ref[rows, he * LANES:(he + 1) * LANES] = qa_e.astype(BF16)
            qa_ref[rows, ho * LANES:(ho + 1) * LANES] = qa_o.astype(BF16)
            ka_ref[rows, he * LANES:(he + 1) * LANES] = ka_e.astype(BF16)
            ka_ref[rows, ho * LANES:(ho + 1) * LANES] = ka_o.astype(BF16)
    carry_ref[0:1, :] = base + off


def _attn_proj(x, g, wq, wk, wv, wf, bf, tm, tiles_per_batch=None, l_real=None):
    rows, d = x.shape
    row_spec = pl.BlockSpec((tm, d), lambda i: (i, 0))
    lf_spec = pl.BlockSpec((tm, N_HEADS), lambda i: (i, 0))
    in_specs = [row_spec, _const_spec((1, d)), _const_spec((d, d)), _const_spec((d, d)),
                _const_spec((d, d)), _const_spec((d, LANES)), _const_spec((1, LANES))]
    f32_rows = jax.ShapeDtypeStruct((rows, d), F32)
    lf_shape = jax.ShapeDtypeStruct((rows, N_HEADS), F32)
    if tiles_per_batch is None:
        return pl.pallas_call(
            _attn_proj_sample_kernel,
            grid=(rows // tm,),
            in_specs=in_specs,
            out_specs=[row_spec, row_spec, row_spec, lf_spec],
            out_shape=[f32_rows, f32_rows, f32_rows, lf_shape],
            compiler_params=_params(("parallel",)),
            name="attn_proj_sample",
        )(x, g, wq, wk, wv, wf, bf)
    assert tm == CHUNK
    aug_spec = pl.BlockSpec((tm, 2 * d), lambda i: (i, 0))
    n_tiles = rows // tm
    n_batch = n_tiles // tiles_per_batch
    vt_spec = pl.BlockSpec((None, d, tm), lambda i: (i // tiles_per_batch, 0, i % tiles_per_batch))
    t_shape = jax.ShapeDtypeStruct((n_batch, d, l_real), F32)
    return pl.pallas_call(
        functools.partial(_attn_proj_prompt_kernel, tiles_per_batch=tiles_per_batch),
        grid=(n_tiles,),
        in_specs=in_specs,
        out_specs=[vt_spec, vt_spec, lf_spec, aug_spec, aug_spec, vt_spec,
                   pl.BlockSpec((1, SUBLANES, LANES), lambda i: (i, 0, 0))],
        out_shape=[t_shape, t_shape, lf_shape,
                   jax.ShapeDtypeStruct((rows, 2 * d), BF16),
                   jax.ShapeDtypeStruct((rows, 2 * d), BF16),
                   jax.ShapeDtypeStruct((n_batch, d, tiles_per_batch * tm), BF16),
                   jax.ShapeDtypeStruct((n_tiles, SUBLANES, LANES), F32)],
        scratch_shapes=[pltpu.VMEM((SUBLANES, LANES), F32)],
        compiler_params=_params(("arbitrary",)),
        name="attn_proj_prompt",
    )(x, g, wq, wk, wv, wf, bf)


def _prompt_attn_kernel(cb_ref, qa_ref, ka_ref, vt_ref, o_ref, p_ref, acc_ref, st_ref):
    b = pl.program_id(0)
    hg = pl.program_id(1)
    i = pl.program_id(2)
    nt = pl.num_programs(2)
    hs = p_ref.shape[0]
    rows = [(b * N_HEADS + hg * hs + h) * nt for h in range(hs)]
    ones_rows = jnp.where(lax.broadcasted_iota(jnp.int32, (V_ROWS - HEAD_DIM, CHUNK), 0) == 0,
                          1.0, 0.0).astype(BF16)

    def scores(c, h):
        kj = ka_ref[pl.ds(pl.multiple_of(c * CHUNK, CHUNK), CHUNK), h * LANES:(h + 1) * LANES]
        return _dot_nt(kj, qa_ref[:, h * LANES:(h + 1) * LANES])

    def update_acc(c, h):
        vt = vt_ref[h * HEAD_DIM:(h + 1) * HEAD_DIM, pl.ds(pl.multiple_of(c * CHUNK, CHUNK), CHUNK)]
        pv = _dot(jnp.concatenate([vt, ones_rows], axis=0), p_ref[h])
        acc_ref[h] = st_ref[hs + h:hs + h + 1, :] * acc_ref[h] + pv

    causal = (lax.broadcasted_iota(jnp.int32, (CHUNK, CHUNK), 0)
              <= lax.broadcasted_iota(jnp.int32, (CHUNK, CHUNK), 1))
    for h in range(hs):
        s = jnp.where(causal, scores(i, h), NEG)
        m = jnp.max(s, axis=0, keepdims=True)
        p_ref[h] = jnp.exp2(s - m).astype(BF16)
        st_ref[h:h + 1, :] = m
        st_ref[hs + h:hs + h + 1, :] = jnp.ones((1, CHUNK), F32)
    acc_ref[...] = jnp.zeros_like(acc_ref)

    def step(c, carry):
        pending = jnp.where(c == 0, i, c - 1)
        for h in range(hs):
            s = scores(c, h)
            update_acc(pending, h)
            d = cb_ref[rows[h] + i] - cb_ref[rows[h] + c]
            m = st_ref[h:h + 1, :]
            mn = jnp.maximum(m, jnp.max(s, axis=0, keepdims=True) + d)
            p_ref[h] = jnp.exp2(s - (mn - d)).astype(BF16)
            st_ref[h:h + 1, :] = mn
            st_ref[hs + h:hs + h + 1, :] = jnp.exp2(m - mn)
        return carry

    lax.fori_loop(0, i, step, 0)
    pending = jnp.maximum(i - 1, 0)
    out_t = []
    for h in range(hs):
        update_acc(pending, h)
        out_t.append(acc_ref[h, 0:HEAD_DIM, :] * (1.0 / acc_ref[h, HEAD_DIM:HEAD_DIM + 1, :]))
    o_ref[...] = jnp.concatenate(out_t, axis=0).T.astype(BF16)


def _prompt_attn(cb, qa, ka, vt, n_batch, lp):
    nt = lp // CHUNK
    d = vt.shape[1]
    hs = HEADS_PER_STEP
    ka3 = ka.reshape(n_batch, lp, 2 * d)
    return pl.pallas_call(
        _prompt_attn_kernel,
        grid=(n_batch, N_HEADS // hs, nt),
        in_specs=[pl.BlockSpec(memory_space=pltpu.SMEM),
                  pl.BlockSpec((CHUNK, hs * LANES), lambda b, hg, i: (b * nt + i, hg)),
                  pl.BlockSpec((None, lp, hs * LANES), lambda b, hg, i: (b, 0, hg)),
                  pl.BlockSpec((None, hs * HEAD_DIM, lp), lambda b, hg, i: (b, hg, 0))],
        out_specs=pl.BlockSpec((CHUNK, hs * HEAD_DIM), lambda b, hg, i: (b * nt + i, hg)),
        out_shape=jax.ShapeDtypeStruct((n_batch * lp, d), BF16),
        scratch_shapes=[pltpu.VMEM((hs, CHUNK, CHUNK), BF16),
                        pltpu.VMEM((hs, V_ROWS, CHUNK), F32),
                        pltpu.VMEM((2 * hs, CHUNK), F32)],
        compiler_params=_params(("parallel", "parallel", "parallel")),
        name="prompt_attn",
    )(cb, qa, ka3, vt)


def _decode_attn_kernel(pt_ref, q_ref, kn_ref, vn_ref, lfn_ref, *rest, n_pages_step, n_new):
    np_ = n_pages_step
    k_refs = rest[0:np_]
    v_refs = rest[np_:2 * np_]
    lf_refs = rest[2 * np_:3 * np_]
    o_ref = rest[3 * np_]
    qbd_ref, fc_ref, ct_ref, m_ref, l_ref, acc_ref, kpad_ref, vpad_ref = rest[3 * np_ + 1:]
    s_idx = pl.program_id(1)
    rows = n_new * N_HEADS
    d = q_ref.shape[-1]
    lane = lax.broadcasted_iota(jnp.int32, (rows, LANES), 1)
    row = lax.broadcasted_iota(jnp.int32, (rows, LANES), 0)
    t_of_row = _div_pow2(row, N_HEADS)

    def tile_rows(x):
        return jnp.concatenate([x] * n_new, axis=0)

    def own_head():
        head_of_lane = _div_pow2(lax.broadcasted_iota(jnp.int32, (rows, d), 1), HEAD_DIM)
        head_of_row = _mod_pow2(lax.broadcasted_iota(jnp.int32, (rows, d), 0), N_HEADS)
        return head_of_lane == head_of_row

    @pl.when(s_idx == 0)
    def _():
        q = q_ref[...] * (HEAD_DIM ** -0.5)
        qb = jnp.concatenate([jnp.broadcast_to(q[t:t + 1, :], (N_HEADS, d)) for t in range(n_new)], axis=0)
        qbd = jnp.where(own_head(), qb, 0.0).astype(BF16)
        qbd_ref[...] = qbd
        kpad_ref[...] = jnp.zeros_like(kpad_ref)
        vpad_ref[...] = jnp.zeros_like(vpad_ref)
        kpad_ref[0:n_new, :] = kn_ref[...]
        vpad_ref[0:n_new, :] = vn_ref[...]
        lfn = lfn_ref[...]
        lane_h = lax.broadcasted_iota(jnp.int32, (N_HEADS, LANES), 1)
        run = jnp.zeros((N_HEADS, 1), F32)
        fnt = jnp.zeros((N_HEADS, LANES), F32)
        cols = []
        for t in range(n_new):
            run = run + lfn[:, t:t + 1]
            cols.append(run)
            fnt = jnp.where(lane_h == t, run, fnt)
        fn_col = jnp.concatenate(cols, axis=0)
        fc_ref[...] = fn_col
        s = _dot_nt(qbd, kpad_ref[...].astype(BF16))
        s = s + fn_col - tile_rows(fnt)
        s = jnp.where(lane <= t_of_row, s, NEG)
        m = jnp.max(s, axis=-1, keepdims=True)
        p = jnp.exp(s - m)
        m_ref[...] = m
        l_ref[...] = jnp.sum(p, axis=-1, keepdims=True)
        acc_ref[...] = _dot(p.astype(BF16), vpad_ref[...].astype(BF16))
        ct_ref[...] = jnp.zeros_like(ct_ref)

    qbd = qbd_ref[...]
    strict_lower = (lax.broadcasted_iota(jnp.int32, (LANES, LANES), 0)
                    > lax.broadcasted_iota(jnp.int32, (LANES, LANES), 1)).astype(BF16)
    ct = ct_ref[...]
    fn_col = fc_ref[...]
    scores = []
    for p_i in range(np_):
        lft = lf_refs[p_i][...]
        hi, mid, lo = _split3(lft)
        gloc = _dot(hi, strict_lower) + _dot(mid, strict_lower) + _dot(lo, strict_lower)
        s = _dot(qbd, k_refs[p_i][...].astype(BF16))
        scores.append(s + tile_rows(gloc) + (fn_col + tile_rows(ct)))
        ct = ct + jnp.sum(lft, axis=-1, keepdims=True)
    ct_ref[...] = ct
    m_old = m_ref[...]
    m_new = m_old
    for s in scores:
        m_new = jnp.maximum(m_new, jnp.max(s, axis=-1, keepdims=True))
    alpha = jnp.exp(m_old - m_new)
    l_new = alpha * l_ref[...]
    acc = alpha * acc_ref[...]
    for p_i in range(np_):
        p = jnp.exp(scores[p_i] - m_new)
        l_new = l_new + jnp.sum(p, axis=-1, keepdims=True)
        acc = acc + _dot_nt(p.astype(BF16), v_refs[p_i][...].astype(BF16))
    m_ref[...] = m_new
    l_ref[...] = l_new
    acc_ref[...] = acc

    @pl.when(s_idx == pl.num_programs(1) - 1)
    def _():
        o = jnp.where(own_head(), acc * (1.0 / l_new), 0.0)
        for t in range(n_new):
            o_ref[t:t + 1, :] = jnp.sum(o[t * N_HEADS:(t + 1) * N_HEADS, :], axis=0, keepdims=True)


def _decode_attn(page_table, q, kn, vn, lfn_t, cache_kt, cache_vt, cache_lf_t, n_pages_step):
    db, n_new, d = q.shape
    n_pages = page_table.shape[1]
    page = cache_kt.shape[2]
    np_ = n_pages_step
    assert n_pages % np_ == 0 and page == LANES
    steps = n_pages // np_
    rows = n_new * N_HEADS

    def page_map(p_i):
        return lambda b, s, pt: (pt[b, n_pages - 1 - (s * np_ + p_i)], 0, 0)

    new_spec = pl.BlockSpec((None, n_new, d), lambda b, s, pt: (b, 0, 0))
    in_specs = ([new_spec, new_spec, new_spec,
                 pl.BlockSpec((None, N_HEADS, LANES), lambda b, s, pt: (b, 0, 0))]
                + [pl.BlockSpec((None, d, page), page_map(p_i)) for p_i in range(np_)]
                + [pl.BlockSpec((None, d, page), page_map(p_i)) for p_i in range(np_)]
                + [pl.BlockSpec((None, N_HEADS, page), page_map(p_i)) for p_i in range(np_)])
    grid_spec = pltpu.PrefetchScalarGridSpec(
        num_scalar_prefetch=1,
        grid=(db, steps),
        in_specs=in_specs,
        out_specs=pl.BlockSpec((None, n_new, d), lambda b, s, pt: (b, 0, 0)),
        scratch_shapes=[pltpu.VMEM((rows, d), BF16),
                        pltpu.VMEM((rows, 1), F32),
                        pltpu.VMEM((N_HEADS, 1), F32),
                        pltpu.VMEM((rows, 1), F32),
                        pltpu.VMEM((rows, 1), F32),
                        pltpu.VMEM((rows, d), F32),
                        pltpu.VMEM((LANES, d), F32),
                        pltpu.VMEM((LANES, d), F32)])
    return pl.pallas_call(
        functools.partial(_decode_attn_kernel, n_pages_step=np_, n_new=n_new),
        grid_spec=grid_spec,
        out_shape=jax.ShapeDtypeStruct((db, n_new, d), F32),
        compiler_params=_params(("parallel", "arbitrary")),
        name="decode_attn",
    )(page_table, q, kn, vn, lfn_t, *([cache_kt] * np_), *([cache_vt] * np_), *([cache_lf_t] * np_))


def _out_proj_kernel(x_ref, o_ref, w_ref, g_ref, y_ref):
    m = _dot(o_ref[...].astype(BF16), w_ref[...])
    y_ref[...] = x_ref[...] + _rms(m, g_ref[...])


def _out_proj(x, o, w, g, tm):
    rows, d = x.shape
    row_spec = pl.BlockSpec((tm, d), lambda i: (i, 0))
    return pl.pallas_call(
        _out_proj_kernel,
        grid=(rows // tm,),
        in_specs=[row_spec, row_spec, _const_spec((d, d)), _const_spec((1, d))],
        out_specs=row_spec,
        out_shape=jax.ShapeDtypeStruct((rows, d), F32),
        compiler_params=_params(("parallel",)),
        name="mixer_out_proj",
    )(x, o, w, g)


def _conv_core(x_ref, gpre_ref, wb_ref, wc_ref, wh_ref):
    hn = _rms(x_ref[...], gpre_ref[...]).astype(BF16)
    return _dot(hn, wb_ref[...]), _dot(hn, wc_ref[...]) * _dot(hn, wh_ref[...])


def _conv_finish(x_ref, gate, conv, wo_ref, gpost_ref, y_ref):
    m = _dot((gate * conv).astype(BF16), wo_ref[...])
    y_ref[...] = x_ref[...] + _rms(m, gpost_ref[...])


def _conv_prompt_kernel(x_ref, gpre_ref, wb_ref, wc_ref, wh_ref, ck_ref, wo_ref, gpost_ref,
                        y_ref, tail_ref, u_ref, *, tiles_per_batch, pad_rows):
    i = pl.program_id(0)
    tm = x_ref.shape[0]
    hist = SUBLANES

    @pl.when(i == 0)
    def _():
        u_ref[0:hist, :] = jnp.zeros((hist, u_ref.shape[1]), F32)

    @pl.when(i > 0)
    def _():
        u_ref[0:hist, :] = u_ref[tm:tm + hist, :]

    gate, u = _conv_core(x_ref, gpre_ref, wb_ref, wc_ref, wh_ref)
    last = i % tiles_per_batch == tiles_per_batch - 1
    n_real = jnp.where(last, tm - pad_rows, tm)
    u = jnp.where(lax.broadcasted_iota(jnp.int32, (tm, 1), 0) < n_real, u, 0.0)
    u_ref[hist:hist + tm, :] = u
    tail_ref[0] = u[tm - pad_rows - hist:tm - pad_rows, :]
    ck = ck_ref[...]
    conv = (ck[0:1, :] * u_ref[hist - 2:hist - 2 + tm, :]
            + ck[1:2, :] * u_ref[hist - 1:hist - 1 + tm, :]
            + ck[2:3, :] * u)
    _conv_finish(x_ref, gate, conv, wo_ref, gpost_ref, y_ref)


def _conv_sample_kernel(x_ref, gpre_ref, wb_ref, wc_ref, wh_ref, ck_ref, wo_ref, gpost_ref,
                        h1_ref, h2_ref, y_ref, uo_ref, u_ref, *, n_new):
    tm = x_ref.shape[0]
    hist = SUBLANES
    gate, u = _conv_core(x_ref, gpre_ref, wb_ref, wc_ref, wh_ref)
    uo_ref[...] = u
    u_ref[0:hist, :] = jnp.zeros((hist, u_ref.shape[1]), F32)
    u_ref[hist:hist + tm, :] = u
    t = _mod_pow2(lax.broadcasted_iota(jnp.int32, (tm, 1), 0), n_new)
    ck = ck_ref[...]
    conv = (ck[0:1, :] * jnp.where(t >= 2, u_ref[hist - 2:hist - 2 + tm, :], h2_ref[...])
            + ck[1:2, :] * jnp.where(t >= 1, u_ref[hist - 1:hist - 1 + tm, :], h1_ref[...])
            + ck[2:3, :] * u)
    _conv_finish(x_ref, gate, conv, wo_ref, gpost_ref, y_ref)


def _conv_mixer(x, g_pre, wb, wc, wh, ck, wo, g_post, tm, tiles_per_batch=None, pad_rows=None, hist1=None,
                hist2=None, n_new=None):
    rows, d = x.shape
    row_spec = pl.BlockSpec((tm, d), lambda i: (i, 0))
    w_spec = _const_spec((d, d))
    in_specs = [row_spec, _const_spec((1, d)), w_spec, w_spec, w_spec, _const_spec((CONV_W, d)),
                w_spec, _const_spec((1, d))]
    f32_rows = jax.ShapeDtypeStruct((rows, d), F32)
    scratch = [pltpu.VMEM((tm + 2 * SUBLANES, d), F32)]
    if tiles_per_batch is None:
        return pl.pallas_call(
            functools.partial(_conv_sample_kernel, n_new=n_new),
            grid=(rows // tm,),
            in_specs=in_specs + [row_spec, row_spec],
            out_specs=[row_spec, row_spec],
            out_shape=[f32_rows, f32_rows],
            scratch_shapes=scratch,
            compiler_params=_params(("parallel",)),
            name="conv_mixer_sample",
        )(x, g_pre, wb, wc, wh, ck, wo, g_post, hist1, hist2)
    n_tiles = rows // tm
    assert pad_rows % SUBLANES == 0 and SUBLANES <= tm - pad_rows
    return pl.pallas_call(
        functools.partial(_conv_prompt_kernel, tiles_per_batch=tiles_per_batch, pad_rows=pad_rows),
        grid=(n_tiles,),
        in_specs=in_specs,
        out_specs=[row_spec, pl.BlockSpec((1, SUBLANES, d), lambda i: (i, 0, 0))],
        out_shape=[f32_rows, jax.ShapeDtypeStruct((n_tiles, SUBLANES, d), F32)],
        scratch_shapes=scratch,
        compiler_params=_params(("arbitrary",)),
        name="conv_mixer_prompt",
    )(x, g_pre, wb, wc, wh, ck, wo, g_post)


def kernel(x_prompt, x_sample, cache_k, cache_v, cache_logf, state_conv, page_table, meta_tokens, norm_g,
           ffn_w_gate, ffn_w_up, ffn_w_down, attn_w_in, attn_b_f, attn_w_out, conv_w_in, conv_kernel,
           conv_w_out):
    n_batch, seq, d = x_prompt.shape
    db, n_new, _ = x_sample.shape
    depth = norm_g.shape[0]
    n_pool, page = cache_k.shape[1], cache_k.shape[2]
    l_real = N_META + seq
    pad_rows = (-l_real) % CHUNK
    lp = l_real + pad_rows
    tpb = lp // CHUNK
    tm_p = CHUNK
    tm_r = ROW_TILE if (n_batch * lp) % ROW_TILE == 0 else CHUNK
    tm_s = db * n_new
    assert d == D_MODEL and page == LANES and tm_s % SUBLANES == 0

    meta = jnp.broadcast_to(meta_tokens[None].astype(F32), (n_batch, N_META, d))
    xp = jnp.concatenate([meta, x_prompt, jnp.zeros((n_batch, pad_rows, d), F32)], axis=1).reshape(n_batch * lp, d)
    xs = x_sample.reshape(tm_s, d)

    def row(v):
        return v.reshape(1, -1).astype(F32)

    k_p, v_p, lf_p, cv_p, k_s, v_s, lf_s, cv_s = [], [], [], [], [], [], [], []
    for i in range(depth):
        g = norm_g[i]
        wg, wu, wd = (w[i].astype(BF16) for w in (ffn_w_gate, ffn_w_up, ffn_w_down))
        xp = _ffn(xp, row(g[0]), row(g[1]), wg[0], wu[0], wd[0], tm_r)
        xs = _ffn(xs, row(g[0]), row(g[1]), wg[0], wu[0], wd[0], tm_s)
        if i % 2 == 0:
            a = i // 2
            w_in = attn_w_in[a]
            wq, wk, wv = (w_in[:, j * d:(j + 1) * d].astype(BF16) for j in range(3))
            wf = jnp.pad(w_in[:, 3 * d:], ((0, 0), (0, LANES - N_HEADS))).astype(BF16)
            bf = jnp.pad(attn_b_f[a], (0, LANES - N_HEADS)).reshape(1, LANES).astype(F32)
            wo = attn_w_out[a].astype(BF16)

            kt, vt32, lfp, qa, ka, vt, cb = _attn_proj(xp, row(g[2]), wq, wk, wv, wf, bf, tm_p, tpb, l_real)
            cb = cb[:, 0, :N_HEADS].reshape(n_batch, tpb, N_HEADS)
            cb = cb.transpose(0, 2, 1).reshape(n_batch * N_HEADS * tpb)
            op = _prompt_attn(cb, qa, ka, vt, n_batch, lp)
            xp = _out_proj(xp, op, wo, row(g[3]), tm_r)

            qs, ks, vs, lfs = _attn_proj(xs, row(g[2]), wq, wk, wv, wf, bf, tm_s)
            lfn_t = jnp.pad(lfs.reshape(db, n_new, N_HEADS).transpose(0, 2, 1),
                            ((0, 0), (0, 0), (0, LANES - n_new)))
            ck_t = cache_k[a].transpose(0, 2, 3, 1).reshape(n_pool, d, page)
            cv_t = cache_v[a].transpose(0, 2, 3, 1).reshape(n_pool, d, page)
            clf_t = cache_logf[a].astype(F32).transpose(0, 2, 1)
            osm = _decode_attn(page_table, qs.reshape(db, n_new, d), ks.reshape(db, n_new, d),
                               vs.reshape(db, n_new, d), lfn_t, ck_t, cv_t, clf_t, 8)
            xs = _out_proj(xs, osm.reshape(tm_s, d), wo, row(g[3]), tm_s)

            k_p.append(kt.reshape(n_batch, N_HEADS, HEAD_DIM, l_real).transpose(0, 3, 1, 2))
            v_p.append(vt32.reshape(n_batch, N_HEADS, HEAD_DIM, l_real).transpose(0, 3, 1, 2))
            lf_p.append(lfp.reshape(n_batch, lp, N_HEADS)[:, :l_real])
            k_s.append(ks.reshape(db, n_new, N_HEADS, HEAD_DIM))
            v_s.append(vs.reshape(db, n_new, N_HEADS, HEAD_DIM))
            lf_s.append(lfs.reshape(db, n_new, N_HEADS))
        else:
            c = i // 2
            w_in = conv_w_in[c]
            wb, wc, wh = (w_in[:, j * d:(j + 1) * d].astype(BF16) for j in range(3))
            wo = conv_w_out[c].astype(BF16)
            ck = conv_kernel[c].astype(F32)
            xp, tails = _conv_mixer(xp, row(g[2]), wb, wc, wh, ck, wo, row(g[3]), tm_p, tiles_per_batch=tpb,
                                    pad_rows=pad_rows)
            tails = tails.reshape(n_batch, tpb, SUBLANES, d)
            cv_p.append(tails[:, tpb - 1, SUBLANES - (CONV_W - 1):])

            st = state_conv[c].astype(F32)
            zero = jnp.zeros((db, n_new, d), F32)
            hist1 = zero.at[:, 0].set(st[:, 1]).reshape(tm_s, d)
            hist2 = zero.at[:, 0].set(st[:, 0]).at[:, 1].set(st[:, 1]).reshape(tm_s, d)
            xs, us = _conv_mixer(xs, row(g[2]), wb, wc, wh, ck, wo, row(g[3]), tm_s, hist1=hist1,
                                 hist2=hist2, n_new=n_new)
            us_ext = jnp.concatenate([st, us.reshape(db, n_new, d)], axis=1)
            cv_s.append(us_ext[:, -(CONV_W - 1):])
        xp = _ffn(xp, row(g[4]), row(g[5]), wg[1], wu[1], wd[1], tm_r)
        xs = _ffn(xs, row(g[4]), row(g[5]), wg[1], wu[1], wd[1], tm_s)

    y_prompt = xp.reshape(n_batch, lp, d)[:, N_META:l_real]
    y_sample = xs.reshape(db, n_new, d)
    return (y_prompt, y_sample, jnp.stack(k_p), jnp.stack(v_p), jnp.stack(lf_p), jnp.stack(cv_p),
            jnp.stack(k_s), jnp.stack(v_s), jnp.stack(lf_s), jnp.stack(cv_s))
```

```python
import functools

import jax
import jax.numpy as jnp
from jax import lax
from jax.experimental import pallas as pl
from jax.experimental.pallas import tpu as pltpu

D_MODEL = 1024
N_HEADS = 16
HEAD_DIM = 64
N_META = 16
CONV_W = 3
NORM_EPS = 1e-6
HALF = 0.5

LANES = 128
SUBLANES = 8
BLK = 128
CHUNK = 2 * BLK
ROW_TILE = 2 * CHUNK
V_ROWS = HEAD_DIM + 16
HEADS_PER_STEP = 16
DECODE_PAGES_PER_STEP = 16
FF_CHUNK = 256
VMEM_LIMIT = 56 * 1024 * 1024
NEG = -1e30
LOG2E = 1.4426950408889634

F32 = jnp.float32
BF16 = jnp.bfloat16


def _dot(a, b):
    return jnp.dot(a, b, preferred_element_type=F32)


def _dot_nt(a, b):
    return lax.dot_general(a, b, (((1,), (1,)), ((), ())), preferred_element_type=F32)


def _div_pow2(x, n):
    assert n & (n - 1) == 0
    return lax.shift_right_logical(x, n.bit_length() - 1)


def _mod_pow2(x, n):
    assert n & (n - 1) == 0
    return x & (n - 1)


def _rms(x, g):
    ms = jnp.mean(x * x, axis=-1, keepdims=True)
    return x * lax.rsqrt(ms + NORM_EPS) * g


def _split3(x):
    hi = x.astype(BF16)
    r = x - hi.astype(F32)
    mid = r.astype(BF16)
    lo = (r - mid.astype(F32)).astype(BF16)
    return hi, mid, lo


def _const_spec(shape):
    return pl.BlockSpec(shape, lambda *_: (0,) * len(shape), pipeline_mode=pl.Buffered(1))


def _params(sem):
    return pltpu.CompilerParams(dimension_semantics=sem, vmem_limit_bytes=VMEM_LIMIT)


def _ffn_kernel(*refs, with_mixer):
    if with_mixer:
        x_ref, mix_ref, wmix_ref, gmix_ref = refs[:4]
        refs = refs[:1] + refs[4:]
    x_ref, gpre_ref, gpost_ref, wg_ref, wu_ref, wd_ref, o_ref, h_ref = refs
    x = x_ref[...]
    if with_mixer:
        x = x + _rms(_dot(mix_ref[...].astype(BF16), wmix_ref[...]), gmix_ref[...])
    xn = _rms(x, gpre_ref[...]).astype(BF16)
    d_ff = wg_ref.shape[1]
    for c in range(d_ff // FF_CHUNK):
        sl = slice(c * FF_CHUNK, (c + 1) * FF_CHUNK)
        g = _dot(xn, wg_ref[:, sl])
        u = _dot(xn, wu_ref[:, sl])
        h_ref[:, sl] = (g * (1.0 / (1.0 + jnp.exp(-g))) * u).astype(BF16)
    y = _dot(h_ref[...], wd_ref[...])
    o_ref[...] = x + HALF * _rms(y, gpost_ref[...])


def _ffn(x, g_pre, g_post, wg, wu, wd, tm, mixer=None):
    rows, d = x.shape
    d_ff = wg.shape[1]
    assert rows % tm == 0 and d_ff % FF_CHUNK == 0
    row_spec = pl.BlockSpec((tm, d), lambda i: (i, 0))
    mixer_specs = [] if mixer is None else [row_spec, _const_spec((d, d)), _const_spec((1, d))]
    return pl.pallas_call(
        functools.partial(_ffn_kernel, with_mixer=mixer is not None),
        grid=(rows // tm,),
        in_specs=[row_spec] + mixer_specs + [_const_spec((1, d)), _const_spec((1, d)), _const_spec((d, d_ff)),
                                             _const_spec((d, d_ff)), _const_spec((d_ff, d))],
        out_specs=row_spec,
        out_shape=jax.ShapeDtypeStruct((rows, d), F32),
        scratch_shapes=[pltpu.VMEM((tm, d_ff), BF16)],
        compiler_params=_params(("parallel",)),
        name="half_ffn",
    )(x, *(mixer or ()), g_pre, g_post, wg, wu, wd)


def _log_sigmoid(z):
    return jnp.minimum(z, 0.0) - jnp.log1p(jnp.exp(-jnp.abs(z)))


def _attn_proj_core(x_ref, g_ref, wq_ref, wk_ref, wv_ref, wf_ref, bf_ref):
    hn = _rms(x_ref[...], g_ref[...]).astype(BF16)
    q = _dot(hn, wq_ref[...])
    k = _dot(hn, wk_ref[...])
    v = _dot(hn, wv_ref[...])
    lf = _log_sigmoid(_dot(hn, wf_ref[...]) + bf_ref[...])
    return q, k, v, lf


def _attn_proj_sample_kernel(x_ref, g_ref, wq_ref, wk_ref, wv_ref, wf_ref, bf_ref,
                             q_ref, k_ref, v_ref, lf_ref):
    q, k, v, lf = _attn_proj_core(x_ref, g_ref, wq_ref, wk_ref, wv_ref, wf_ref, bf_ref)
    q_ref[...] = q
    k_ref[...] = k
    v_ref[...] = v
    lf_ref[...] = lf[:, :N_HEADS]


def _attn_proj_prompt_kernel(x_ref, g_ref, wq_ref, wk_ref, wv_ref, wf_ref, bf_ref,
                             kt_ref, vt32_ref, lf_ref, qa_ref, ka_ref, vt_ref, cb_ref,
                             carry_ref, *, tiles_per_batch):
    i = pl.program_id(0)
    tm = x_ref.shape[0]
    q, k, v, lf = _attn_proj_core(x_ref, g_ref, wq_ref, wk_ref, wv_ref, wf_ref, bf_ref)
    kt_ref[...] = k.T
    vt = v.T
    vt32_ref[...] = vt
    vt_ref[...] = vt.astype(BF16)
    lf_ref[...] = lf[:, :N_HEADS]

    lane = lax.broadcasted_iota(jnp.int32, (BLK, LANES), 1)
    lf = jnp.where(lax.broadcasted_iota(jnp.int32, lf.shape, 1) < N_HEADS, lf * LOG2E, 0.0)
    tril = (lax.broadcasted_iota(jnp.int32, (BLK, BLK), 0)
            >= lax.broadcasted_iota(jnp.int32, (BLK, BLK), 1)).astype(BF16)
    scale = HEAD_DIM ** -0.5 * LOG2E

    first = i % tiles_per_batch == 0
    base = jnp.where(first, 0.0, carry_ref[0:1, :])
    cb_ref[...] = jnp.zeros_like(cb_ref)
    cb_ref[0, 0:1, :] = base
    off = jnp.zeros((1, LANES), F32)
    for jb in range(tm // BLK):
        rows = slice(jb * BLK, (jb + 1) * BLK)
        hi, mid, lo = _split3(lf[rows])
        floc = _dot(tril, hi) + _dot(tril, mid) + _dot(tril, lo) + off
        off = floc[BLK - 1:BLK, :]

        f_hi = floc.astype(BF16).astype(F32)
        f_lo = floc - f_hi
        fq_odd = f_hi + pltpu.roll(f_lo, N_HEADS, axis=1)
        fq_even = pltpu.roll(fq_odd, 64, axis=1)
        fk_odd = pltpu.roll(-fq_odd, 32, axis=1)
        fk_even = pltpu.roll(fk_odd, 64, axis=1)
        for hp in range(N_HEADS // 2):
            cols = slice(hp * LANES, (hp + 1) * LANES)
            qc = q[rows, cols] * scale
            kc = k[rows, cols]
            he, ho = 2 * hp, 2 * hp + 1
            one_q_even = (lane == 96 + he) | (lane == 112 + he)
            one_q_odd = (lane == 32 + ho) | (lane == 48 + ho)
            one_k_even = (lane == 64 + he) | (lane == 80 + he)
            one_k_odd = (lane == ho) | (lane == 16 + ho)
            low = lane < 64
            qa_e = jnp.where(low, qc, jnp.where(one_q_even, 1.0, fq_even))
            qa_o = jnp.where(low, jnp.where(one_q_odd, 1.0, fq_odd), qc)
            ka_e = jnp.where(low, kc, jnp.where(one_k_even, 1.0, fk_even))
            ka_o = jnp.where(low, jnp.where(one_k_odd, 1.0, fk_odd), kc)
            qa_ref[rows, he * LANES:(he + 1) * LANES] = qa_e.astype(BF16)
            qa_ref[rows, ho * LANES:(ho + 1) * LANES] = qa_o.astype(BF16)
            ka_ref[rows, he * LANES:(he + 1) * LANES] = ka_e.astype(BF16)
            ka_ref[rows, ho * LANES:(ho + 1) * LANES] = ka_o.astype(BF16)
    carry_ref[0:1, :] = base + off


def _attn_proj(x, g, wq, wk, wv, wf, bf, tm, tiles_per_batch=None, l_real=None):
    rows, d = x.shape
    row_spec = pl.BlockSpec((tm, d), lambda i: (i, 0))
    lf_spec = pl.BlockSpec((tm, N_HEADS), lambda i: (i, 0))
    in_specs = [row_spec, _const_spec((1, d)), _const_spec((d, d)), _const_spec((d, d)),
                _const_spec((d, d)), _const_spec((d, LANES)), _const_spec((1, LANES))]
    f32_rows = jax.ShapeDtypeStruct((rows, d), F32)
    lf_shape = jax.ShapeDtypeStruct((rows, N_HEADS), F32)
    if tiles_per_batch is None:
        return pl.pallas_call(
            _attn_proj_sample_kernel,
            grid=(rows // tm,),
            in_specs=in_specs,
            out_specs=[row_spec, row_spec, row_spec, lf_spec],
            out_shape=[f32_rows, f32_rows, f32_rows, lf_shape],
            compiler_params=_params(("parallel",)),
            name="attn_proj_sample",
        )(x, g, wq, wk, wv, wf, bf)
    assert tm == CHUNK
    aug_spec = pl.BlockSpec((tm, 2 * d), lambda i: (i, 0))
    n_tiles = rows // tm
    n_batch = n_tiles // tiles_per_batch
    vt_spec = pl.BlockSpec((None, d, tm), lambda i: (i // tiles_per_batch, 0, i % tiles_per_batch))
    t_shape = jax.ShapeDtypeStruct((n_batch, d, l_real), F32)
    return pl.pallas_call(
        functools.partial(_attn_proj_prompt_kernel, tiles_per_batch=tiles_per_batch),
        grid=(n_tiles,),
        in_specs=in_specs,
        out_specs=[vt_spec, vt_spec, lf_spec, aug_spec, aug_spec, vt_spec,
                   pl.BlockSpec((1, SUBLANES, LANES), lambda i: (i, 0, 0))],
        out_shape=[t_shape, t_shape, lf_shape,
                   jax.ShapeDtypeStruct((rows, 2 * d), BF16),
                   jax.ShapeDtypeStruct((rows, 2 * d), BF16),
                   jax.ShapeDtypeStruct((n_batch, d, tiles_per_batch * tm), BF16),
                   jax.ShapeDtypeStruct((n_tiles, SUBLANES, LANES), F32)],
        scratch_shapes=[pltpu.VMEM((SUBLANES, LANES), F32)],
        compiler_params=_params(("arbitrary",)),
        name="attn_proj_prompt",
    )(x, g, wq, wk, wv, wf, bf)


def _prompt_attn_kernel(cb_ref, qa_ref, ka_ref, vt_ref, o_ref, p_ref, acc_ref, st_ref):
    b = pl.program_id(0)
    hg = pl.program_id(1)
    i = pl.program_id(2)
    nt = pl.num_programs(2)
    hs = p_ref.shape[0]
    rows = [(b * N_HEADS + hg * hs + h) * nt for h in range(hs)]
    ones_rows = jnp.where(lax.broadcasted_iota(jnp.int32, (V_ROWS - HEAD_DIM, CHUNK), 0) == 0,
                          1.0, 0.0).astype(BF16)

    def scores(c, h):
        kj = ka_ref[pl.ds(pl.multiple_of(c * CHUNK, CHUNK), CHUNK), h * LANES:(h + 1) * LANES]
        return _dot_nt(kj, qa_ref[:, h * LANES:(h + 1) * LANES])

    def update_acc(c, h):
        vt = vt_ref[h * HEAD_DIM:(h + 1) * HEAD_DIM, pl.ds(pl.multiple_of(c * CHUNK, CHUNK), CHUNK)]
        pv = _dot(jnp.concatenate([vt, ones_rows], axis=0), p_ref[h])
        acc_ref[h] = st_ref[hs + h:hs + h + 1, :] * acc_ref[h] + pv

    causal = (lax.broadcasted_iota(jnp.int32, (CHUNK, CHUNK), 0)
              <= lax.broadcasted_iota(jnp.int32, (CHUNK, CHUNK), 1))
    for h in range(hs):
        s = jnp.where(causal, scores(i, h), NEG)
        m = jnp.max(s, axis=0, keepdims=True)
        p_ref[h] = jnp.exp2(s - m).astype(BF16)
        st_ref[h:h + 1, :] = m
        st_ref[hs + h:hs + h + 1, :] = jnp.ones((1, CHUNK), F32)
    acc_ref[...] = jnp.zeros_like(acc_ref)

    def step(c, carry):
        pending = jnp.where(c == 0, i, c - 1)
        for h in range(hs):
            s = scores(c, h)
            update_acc(pending, h)
            d = cb_ref[rows[h] + i] - cb_ref[rows[h] + c]
            m = st_ref[h:h + 1, :]
            mn = jnp.maximum(m, jnp.max(s, axis=0, keepdims=True) + d)
            p_ref[h] = jnp.exp2(s - (mn - d)).astype(BF16)
            st_ref[h:h + 1, :] = mn
            st_ref[hs + h:hs + h + 1, :] = jnp.exp2(m - mn)
        return carry

    lax.fori_loop(0, i, step, 0)
    pending = jnp.maximum(i - 1, 0)
    out_t = []
    for h in range(hs):
        update_acc(pending, h)
        out_t.append(acc_ref[h, 0:HEAD_DIM, :] * (1.0 / acc_ref[h, HEAD_DIM:HEAD_DIM + 1, :]))
    o_ref[...] = jnp.concatenate(out_t, axis=0).T.astype(BF16)


def _prompt_attn(cb, qa, ka, vt, n_batch, lp):
    nt = lp // CHUNK
    d = vt.shape[1]
    hs = HEADS_PER_STEP
    ka3 = ka.reshape(n_batch, lp, 2 * d)
    return pl.pallas_call(
        _prompt_attn_kernel,
        grid=(n_batch, N_HEADS // hs, nt),
        in_specs=[pl.BlockSpec(memory_space=pltpu.SMEM),
                  pl.BlockSpec((CHUNK, hs * LANES), lambda b, hg, i: (b * nt + i, hg)),
                  pl.BlockSpec((None, lp, hs * LANES), lambda b, hg, i: (b, 0, hg),
                               pipeline_mode=pl.Buffered(1)),
                  pl.BlockSpec((None, hs * HEAD_DIM, lp), lambda b, hg, i: (b, hg, 0),
                               pipeline_mode=pl.Buffered(1))],
        out_specs=pl.BlockSpec((CHUNK, hs * HEAD_DIM), lambda b, hg, i: (b * nt + i, hg)),
        out_shape=jax.ShapeDtypeStruct((n_batch * lp, d), BF16),
        scratch_shapes=[pltpu.VMEM((hs, CHUNK, CHUNK), BF16),
                        pltpu.VMEM((hs, V_ROWS, CHUNK), F32),
                        pltpu.VMEM((2 * hs, CHUNK), F32)],
        compiler_params=_params(("parallel", "parallel", "parallel")),
        name="prompt_attn",
    )(cb, qa, ka3, vt)


def _decode_attn_kernel(pt_ref, q_ref, kn_ref, vn_ref, lfn_ref, *rest, n_pages_step, n_new):
    np_ = n_pages_step
    k_refs = rest[0:np_]
    v_refs = rest[np_:2 * np_]
    lf_refs = rest[2 * np_:3 * np_]
    o_ref = rest[3 * np_]
    qbd_ref, fc_ref, ct_ref, m_ref, l_ref, acc_ref, kpad_ref, vpad_ref = rest[3 * np_ + 1:]
    s_idx = pl.program_id(1)
    rows = n_new * N_HEADS
    d = q_ref.shape[-1]
    lane = lax.broadcasted_iota(jnp.int32, (rows, LANES), 1)
    row = lax.broadcasted_iota(jnp.int32, (rows, LANES), 0)
    t_of_row = _div_pow2(row, N_HEADS)

    def tile_rows(x):
        return jnp.concatenate([x] * n_new, axis=0)

    def own_head():
        head_of_lane = _div_pow2(lax.broadcasted_iota(jnp.int32, (rows, d), 1), HEAD_DIM)
        head_of_row = _mod_pow2(lax.broadcasted_iota(jnp.int32, (rows, d), 0), N_HEADS)
        return head_of_lane == head_of_row

    @pl.when(s_idx == 0)
    def _():
        q = q_ref[...] * (HEAD_DIM ** -0.5)
        qb = jnp.concatenate([jnp.broadcast_to(q[t:t + 1, :], (N_HEADS, d)) for t in range(n_new)], axis=0)
        qbd = jnp.where(own_head(), qb, 0.0).astype(BF16)
        qbd_ref[...] = qbd
        kpad_ref[...] = jnp.zeros_like(kpad_ref)
        vpad_ref[...] = jnp.zeros_like(vpad_ref)
        kpad_ref[0:n_new, :] = kn_ref[...]
        vpad_ref[0:n_new, :] = vn_ref[...]
        lfn = lfn_ref[...]
        lane_h = lax.broadcasted_iota(jnp.int32, (N_HEADS, LANES), 1)
        run = jnp.zeros((N_HEADS, 1), F32)
        fnt = jnp.zeros((N_HEADS, LANES), F32)
        cols = []
        for t in range(n_new):
            run = run + lfn[:, t:t + 1]
            cols.append(run)
            fnt = jnp.where(lane_h == t, run, fnt)
        fn_col = jnp.concatenate(cols, axis=0)
        fc_ref[...] = fn_col
        s = _dot_nt(qbd, kpad_ref[...].astype(BF16))
        s = s + fn_col - tile_rows(fnt)
        s = jnp.where(lane <= t_of_row, s, NEG)
        m = jnp.max(s, axis=-1, keepdims=True)
        p = jnp.exp(s - m)
        m_ref[...] = m
        l_ref[...] = jnp.sum(p, axis=-1, keepdims=True)
        acc_ref[...] = _dot(p.astype(BF16), vpad_ref[...].astype(BF16))
        ct_ref[...] = jnp.zeros_like(ct_ref)

    qbd = qbd_ref[...]
    strict_lower = (lax.broadcasted_iota(jnp.int32, (LANES, LANES), 0)
                    > lax.broadcasted_iota(jnp.int32, (LANES, LANES), 1)).astype(BF16)
    ct = ct_ref[...]
    fn_col = fc_ref[...]
    scores = []
    for p_i in range(np_):
        lft = lf_refs[p_i][...]
        hi, mid, lo = _split3(lft)
        gloc = _dot(hi, strict_lower) + _dot(mid, strict_lower) + _dot(lo, strict_lower)
        s = _dot(qbd, k_refs[p_i][...].astype(BF16))
        scores.append(s + tile_rows(gloc) + (fn_col + tile_rows(ct)))
        ct = ct + jnp.sum(lft, axis=-1, keepdims=True)
    ct_ref[...] = ct
    m_old = m_ref[...]
    m_new = m_old
    for s in scores:
        m_new = jnp.maximum(m_new, jnp.max(s, axis=-1, keepdims=True))
    alpha = jnp.exp(m_old - m_new)
    l_new = alpha * l_ref[...]
    acc = alpha * acc_ref[...]
    for p_i in range(np_):
        p = jnp.exp(scores[p_i] - m_new)
        l_new = l_new + jnp.sum(p, axis=-1, keepdims=True)
        acc = acc + _dot_nt(p.astype(BF16), v_refs[p_i][...].astype(BF16))
    m_ref[...] = m_new
    l_ref[...] = l_new
    acc_ref[...] = acc

    @pl.when(s_idx == pl.num_programs(1) - 1)
    def _():
        o = jnp.where(own_head(), acc * (1.0 / l_new), 0.0)
        for t in range(n_new):
            o_ref[t:t + 1, :] = jnp.sum(o[t * N_HEADS:(t + 1) * N_HEADS, :], axis=0, keepdims=True)


def _decode_attn(page_table, q, kn, vn, lfn_t, cache_kt, cache_vt, cache_lf_t, n_pages_step):
    db, n_new, d = q.shape
    n_pages = page_table.shape[1]
    page = cache_kt.shape[2]
    np_ = n_pages_step
    assert n_pages % np_ == 0 and page == LANES
    steps = n_pages // np_
    rows = n_new * N_HEADS

    def page_map(p_i):
        return lambda b, s, pt: (pt[b, n_pages - 1 - (s * np_ + p_i)], 0, 0)

    new_spec = pl.BlockSpec((None, n_new, d), lambda b, s, pt: (b, 0, 0))
    in_specs = ([new_spec, new_spec, new_spec,
                 pl.BlockSpec((None, N_HEADS, LANES), lambda b, s, pt: (b, 0, 0))]
                + [pl.BlockSpec((None, d, page), page_map(p_i)) for p_i in range(np_)]
                + [pl.BlockSpec((None, d, page), page_map(p_i)) for p_i in range(np_)]
                + [pl.BlockSpec((None, N_HEADS, page), page_map(p_i)) for p_i in range(np_)])
    grid_spec = pltpu.PrefetchScalarGridSpec(
        num_scalar_prefetch=1,
        grid=(db, steps),
        in_specs=in_specs,
        out_specs=pl.BlockSpec((None, n_new, d), lambda b, s, pt: (b, 0, 0)),
        scratch_shapes=[pltpu.VMEM((rows, d), BF16),
                        pltpu.VMEM((rows, 1), F32),
                        pltpu.VMEM((N_HEADS, 1), F32),
                        pltpu.VMEM((rows, 1), F32),
                        pltpu.VMEM((rows, 1), F32),
                        pltpu.VMEM((rows, d), F32),
                        pltpu.VMEM((LANES, d), F32),
                        pltpu.VMEM((LANES, d), F32)])
    return pl.pallas_call(
        functools.partial(_decode_attn_kernel, n_pages_step=np_, n_new=n_new),
        grid_spec=grid_spec,
        out_shape=jax.ShapeDtypeStruct((db, n_new, d), F32),
        compiler_params=_params(("parallel", "arbitrary")),
        name="decode_attn",
    )(page_table, q, kn, vn, lfn_t, *([cache_kt] * np_), *([cache_vt] * np_), *([cache_lf_t] * np_))


def _conv_core(x_ref, gpre_ref, wb_ref, wc_ref, wh_ref):
    hn = _rms(x_ref[...], gpre_ref[...]).astype(BF16)
    return _dot(hn, wb_ref[...]), _dot(hn, wc_ref[...]) * _dot(hn, wh_ref[...])


def _conv_finish(x_ref, gate, conv, wo_ref, gpost_ref, y_ref):
    m = _dot((gate * conv).astype(BF16), wo_ref[...])
    y_ref[...] = x_ref[...] + _rms(m, gpost_ref[...])


def _conv_prompt_kernel(x_ref, gpre_ref, wb_ref, wc_ref, wh_ref, ck_ref, wo_ref, gpost_ref,
                        y_ref, tail_ref, u_ref, *, tiles_per_batch, pad_rows):
    i = pl.program_id(0)
    tm = x_ref.shape[0]
    hist = SUBLANES

    @pl.when(i == 0)
    def _():
        u_ref[0:hist, :] = jnp.zeros((hist, u_ref.shape[1]), F32)

    @pl.when(i > 0)
    def _():
        u_ref[0:hist, :] = u_ref[tm:tm + hist, :]

    gate, u = _conv_core(x_ref, gpre_ref, wb_ref, wc_ref, wh_ref)
    last = i % tiles_per_batch == tiles_per_batch - 1
    n_real = jnp.where(last, tm - pad_rows, tm)
    u = jnp.where(lax.broadcasted_iota(jnp.int32, (tm, 1), 0) < n_real, u, 0.0)
    u_ref[hist:hist + tm, :] = u
    tail_ref[0] = u[tm - pad_rows - hist:tm - pad_rows, :]
    ck = ck_ref[...]
    conv = (ck[0:1, :] * u_ref[hist - 2:hist - 2 + tm, :]
            + ck[1:2, :] * u_ref[hist - 1:hist - 1 + tm, :]
            + ck[2:3, :] * u)
    _conv_finish(x_ref, gate, conv, wo_ref, gpost_ref, y_ref)


def _conv_sample_kernel(x_ref, gpre_ref, wb_ref, wc_ref, wh_ref, ck_ref, wo_ref, gpost_ref,
                        h1_ref, h2_ref, y_ref, uo_ref, u_ref, *, n_new):
    tm = x_ref.shape[0]
    hist = SUBLANES
    gate, u = _conv_core(x_ref, gpre_ref, wb_ref, wc_ref, wh_ref)
    uo_ref[...] = u
    u_ref[0:hist, :] = jnp.zeros((hist, u_ref.shape[1]), F32)
    u_ref[hist:hist + tm, :] = u
    t = _mod_pow2(lax.broadcasted_iota(jnp.int32, (tm, 1), 0), n_new)
    ck = ck_ref[...]
    conv = (ck[0:1, :] * jnp.where(t >= 2, u_ref[hist - 2:hist - 2 + tm, :], h2_ref[...])
            + ck[1:2, :] * jnp.where(t >= 1, u_ref[hist - 1:hist - 1 + tm, :], h1_ref[...])
            + ck[2:3, :] * u)
    _conv_finish(x_ref, gate, conv, wo_ref, gpost_ref, y_ref)


def _conv_mixer(x, g_pre, wb, wc, wh, ck, wo, g_post, tm, tiles_per_batch=None, pad_rows=None, hist1=None,
                hist2=None, n_new=None):
    rows, d = x.shape
    row_spec = pl.BlockSpec((tm, d), lambda i: (i, 0))
    w_spec = _const_spec((d, d))
    in_specs = [row_spec, _const_spec((1, d)), w_spec, w_spec, w_spec, _const_spec((CONV_W, d)),
                w_spec, _const_spec((1, d))]
    f32_rows = jax.ShapeDtypeStruct((rows, d), F32)
    scratch = [pltpu.VMEM((tm + 2 * SUBLANES, d), F32)]
    if tiles_per_batch is None:
        return pl.pallas_call(
            functools.partial(_conv_sample_kernel, n_new=n_new),
            grid=(rows // tm,),
            in_specs=in_specs + [row_spec, row_spec],
            out_specs=[row_spec, row_spec],
            out_shape=[f32_rows, f32_rows],
            scratch_shapes=scratch,
            compiler_params=_params(("parallel",)),
            name="conv_mixer_sample",
        )(x, g_pre, wb, wc, wh, ck, wo, g_post, hist1, hist2)
    n_tiles = rows // tm
    assert pad_rows % SUBLANES == 0 and SUBLANES <= tm - pad_rows
    return pl.pallas_call(
        functools.partial(_conv_prompt_kernel, tiles_per_batch=tiles_per_batch, pad_rows=pad_rows),
        grid=(n_tiles,),
        in_specs=in_specs,
        out_specs=[row_spec, pl.BlockSpec((1, SUBLANES, d), lambda i: (i, 0, 0))],
        out_shape=[f32_rows, jax.ShapeDtypeStruct((n_tiles, SUBLANES, d), F32)],
        scratch_shapes=scratch,
        compiler_params=_params(("arbitrary",)),
        name="conv_mixer_prompt",
    )(x, g_pre, wb, wc, wh, ck, wo, g_post)


def kernel(x_prompt, x_sample, cache_k, cache_v, cache_logf, state_conv, page_table, meta_tokens, norm_g,
           ffn_w_gate, ffn_w_up, ffn_w_down, attn_w_in, attn_b_f, attn_w_out, conv_w_in, conv_kernel,
           conv_w_out):
    n_batch, seq, d = x_prompt.shape
    db, n_new, _ = x_sample.shape
    depth = norm_g.shape[0]
    n_pool, page = cache_k.shape[1], cache_k.shape[2]
    l_real = N_META + seq
    pad_rows = (-l_real) % CHUNK
    lp = l_real + pad_rows
    tpb = lp // CHUNK
    tm_p = CHUNK
    tm_r = ROW_TILE if (n_batch * lp) % ROW_TILE == 0 else CHUNK
    tm_s = db * n_new
    assert d == D_MODEL and page == LANES and tm_s % SUBLANES == 0

    meta = jnp.broadcast_to(meta_tokens[None].astype(F32), (n_batch, N_META, d))
    xp = jnp.concatenate([meta, x_prompt, jnp.zeros((n_batch, pad_rows, d), F32)], axis=1).reshape(n_batch * lp, d)
    xs = x_sample.reshape(tm_s, d)

    def row(v):
        return v.reshape(1, -1).astype(F32)

    k_p, v_p, lf_p, cv_p, k_s, v_s, lf_s, cv_s = [], [], [], [], [], [], [], []
    for i in range(depth):
        g = norm_g[i]
        wg, wu, wd = (w[i].astype(BF16) for w in (ffn_w_gate, ffn_w_up, ffn_w_down))
        xp = _ffn(xp, row(g[0]), row(g[1]), wg[0], wu[0], wd[0], tm_r)
        xs = _ffn(xs, row(g[0]), row(g[1]), wg[0], wu[0], wd[0], tm_s)
        if i % 2 == 0:
            a = i // 2
            w_in = attn_w_in[a]
            wq, wk, wv = (w_in[:, j * d:(j + 1) * d].astype(BF16) for j in range(3))
            wf = jnp.pad(w_in[:, 3 * d:], ((0, 0), (0, LANES - N_HEADS))).astype(BF16)
            bf = jnp.pad(attn_b_f[a], (0, LANES - N_HEADS)).reshape(1, LANES).astype(F32)
            wo = attn_w_out[a].astype(BF16)

            kt, vt32, lfp, qa, ka, vt, cb = _attn_proj(xp, row(g[2]), wq, wk, wv, wf, bf, tm_p, tpb, l_real)
            cb = cb[:, 0, :N_HEADS].reshape(n_batch, tpb, N_HEADS)
            cb = cb.transpose(0, 2, 1).reshape(n_batch * N_HEADS * tpb)
            op = _prompt_attn(cb, qa, ka, vt, n_batch, lp)
            mix_p = (op, wo, row(g[3]))

            qs, ks, vs, lfs = _attn_proj(xs, row(g[2]), wq, wk, wv, wf, bf, tm_s)
            lfn_t = jnp.pad(lfs.reshape(db, n_new, N_HEADS).transpose(0, 2, 1),
                            ((0, 0), (0, 0), (0, LANES - n_new)))
            ck_t = cache_k[a].transpose(0, 2, 3, 1).reshape(n_pool, d, page)
            cv_t = cache_v[a].transpose(0, 2, 3, 1).reshape(n_pool, d, page)
            clf_t = cache_logf[a].astype(F32).transpose(0, 2, 1)
            osm = _decode_attn(page_table, qs.reshape(db, n_new, d), ks.reshape(db, n_new, d),
                               vs.reshape(db, n_new, d), lfn_t, ck_t, cv_t, clf_t, DECODE_PAGES_PER_STEP)
            mix_s = (osm.reshape(tm_s, d), wo, row(g[3]))

            k_p.append(kt.reshape(n_batch, N_HEADS, HEAD_DIM, l_real).transpose(0, 3, 1, 2))
            v_p.append(vt32.reshape(n_batch, N_HEADS, HEAD_DIM, l_real).transpose(0, 3, 1, 2))
            lf_p.append(lfp.reshape(n_batch, lp, N_HEADS)[:, :l_real])
            k_s.append(ks.reshape(db, n_new, N_HEADS, HEAD_DIM))
            v_s.append(vs.reshape(db, n_new, N_HEADS, HEAD_DIM))
            lf_s.append(lfs.reshape(db, n_new, N_HEADS))
        else:
            c = i // 2
            w_in = conv_w_in[c]
            wb, wc, wh = (w_in[:, j * d:(j + 1) * d].astype(BF16) for j in range(3))
            wo = conv_w_out[c].astype(BF16)
            ck = conv_kernel[c].astype(F32)
            xp, tails = _conv_mixer(xp, row(g[2]), wb, wc, wh, ck, wo, row(g[3]), tm_p, tiles_per_batch=tpb,
                                    pad_rows=pad_rows)
            tails = tails.reshape(n_batch, tpb, SUBLANES, d)
            cv_p.append(tails[:, tpb - 1, SUBLANES - (CONV_W - 1):])

            st = state_conv[c].astype(F32)
            zero = jnp.zeros((db, n_new, d), F32)
            hist1 = zero.at[:, 0].set(st[:, 1]).reshape(tm_s, d)
            hist2 = zero.at[:, 0].set(st[:, 0]).at[:, 1].set(st[:, 1]).reshape(tm_s, d)
            xs, us = _conv_mixer(xs, row(g[2]), wb, wc, wh, ck, wo, row(g[3]), tm_s, hist1=hist1,
                                 hist2=hist2, n_new=n_new)
            us_ext = jnp.concatenate([st, us.reshape(db, n_new, d)], axis=1)
            cv_s.append(us_ext[:, -(CONV_W - 1):])
            mix_p = mix_s = None
        xp = _ffn(xp, row(g[4]), row(g[5]), wg[1], wu[1], wd[1], tm_r, mix_p)
        xs = _ffn(xs, row(g[4]), row(g[5]), wg[1], wu[1], wd[1], tm_s, mix_s)

    y_prompt = xp.reshape(n_batch, lp, d)[:, N_META:l_real]
    y_sample = xs.reshape(db, n_new, d)
    return (y_prompt, y_sample, jnp.stack(k_p), jnp.stack(v_p), jnp.stack(lf_p), jnp.stack(cv_p),
            jnp.stack(k_s), jnp.stack(v_s), jnp.stack(lf_s), jnp.stack(cv_s))
```

```python
import functools

import jax
import jax.numpy as jnp
from jax import lax
from jax.experimental import pallas as pl
from jax.experimental.pallas import tpu as pltpu

D_MODEL = 1024
N_HEADS = 16
HEAD_DIM = 64
N_META = 16
CONV_W = 3
NORM_EPS = 1e-6
HALF = 0.5

LANES = 128
SUBLANES = 8
BLK = 128
CHUNK = 2 * BLK
ROW_TILE = 4 * CHUNK
V_ROWS = HEAD_DIM + 16
HEADS_PER_STEP = 16
DECODE_PAGES_PER_STEP = 16
FF_CHUNK = 256
VMEM_LIMIT = 56 * 1024 * 1024
NEG = -1e30
LOG2E = 1.4426950408889634

F32 = jnp.float32
BF16 = jnp.bfloat16


def _dot(a, b):
    return jnp.dot(a, b, preferred_element_type=F32)


def _dot_nt(a, b):
    return lax.dot_general(a, b, (((1,), (1,)), ((), ())), preferred_element_type=F32)


def _div_pow2(x, n):
    assert n & (n - 1) == 0
    return lax.shift_right_logical(x, n.bit_length() - 1)


def _mod_pow2(x, n):
    assert n & (n - 1) == 0
    return x & (n - 1)


def _rms(x, g):
    ms = jnp.mean(x * x, axis=-1, keepdims=True)
    return x * lax.rsqrt(ms + NORM_EPS) * g


def _split3(x):
    hi = x.astype(BF16)
    r = x - hi.astype(F32)
    mid = r.astype(BF16)
    lo = (r - mid.astype(F32)).astype(BF16)
    return hi, mid, lo


def _const_spec(shape):
    return pl.BlockSpec(shape, lambda *_: (0,) * len(shape), pipeline_mode=pl.Buffered(1))


def _params(sem):
    return pltpu.CompilerParams(dimension_semantics=sem, vmem_limit_bytes=VMEM_LIMIT)


def _ffn_kernel(*refs, with_mixer):
    if with_mixer:
        x_ref, mix_ref, wmix_ref, gmix_ref = refs[:4]
        refs = refs[:1] + refs[4:]
    x_ref, gpre_ref, gpost_ref, wg_ref, wu_ref, wd_ref, o_ref, h_ref = refs
    x = x_ref[...]
    if with_mixer:
        x = x + _rms(_dot(mix_ref[...].astype(BF16), wmix_ref[...]), gmix_ref[...])
    xn = _rms(x, gpre_ref[...]).astype(BF16)
    d_ff = wg_ref.shape[1]
    for c in range(d_ff // FF_CHUNK):
        sl = slice(c * FF_CHUNK, (c + 1) * FF_CHUNK)
        g = _dot(xn, wg_ref[:, sl])
        u = _dot(xn, wu_ref[:, sl])
        h_ref[:, sl] = (g * (1.0 / (1.0 + jnp.exp(-g))) * u).astype(BF16)
    y = _dot(h_ref[...], wd_ref[...])
    o_ref[...] = x + HALF * _rms(y, gpost_ref[...])


def _ffn(x, g_pre, g_post, wg, wu, wd, tm, mixer=None, window=None):
    rows, d = x.shape
    d_ff = wg.shape[1]
    assert d_ff % FF_CHUNK == 0
    w_specs = [_const_spec((1, d)), _const_spec((1, d)), _const_spec((d, d_ff)), _const_spec((d, d_ff)),
               _const_spec((d_ff, d))]
    if window is None:
        assert rows % tm == 0
        grid = (rows // tm,)
        row_spec = out_spec = pl.BlockSpec((tm, d), lambda i: (i, 0))
        out_shape = jax.ShapeDtypeStruct((rows, d), F32)
    else:
        n_batch, row0, n_rows = window
        assert mixer is None and n_rows % tm == 0 and row0 % SUBLANES == 0
        group = rows // n_batch
        grid = (n_batch, n_rows // tm)
        assert group % SUBLANES == 0 and tm % SUBLANES == 0
        row_spec = pl.BlockSpec((pl.Element(tm), pl.Element(d)),
                                lambda b, t: (pl.multiple_of(b * group + row0 + t * tm, SUBLANES), 0))
        out_spec = pl.BlockSpec((None, tm, d), lambda b, t: (b, t, 0))
        out_shape = jax.ShapeDtypeStruct((n_batch, n_rows, d), F32)
    mixer_specs = [] if mixer is None else [row_spec, _const_spec((d, d)), _const_spec((1, d))]
    return pl.pallas_call(
        functools.partial(_ffn_kernel, with_mixer=mixer is not None),
        grid=grid,
        in_specs=[row_spec] + mixer_specs + w_specs,
        out_specs=out_spec,
        out_shape=out_shape,
        scratch_shapes=[pltpu.VMEM((tm, d_ff), BF16)],
        compiler_params=_params(("parallel",) * len(grid)),
        name="half_ffn",
    )(x, *(mixer or ()), g_pre, g_post, wg, wu, wd)


def _log_sigmoid(z):
    return jnp.minimum(z, 0.0) - jnp.log1p(jnp.exp(-jnp.abs(z)))


def _attn_proj_core(x_ref, g_ref, wq_ref, wk_ref, wv_ref, wf_ref, bf_ref):
    hn = _rms(x_ref[...], g_ref[...]).astype(BF16)
    lf = _log_sigmoid(_dot(hn, wf_ref[...]) + bf_ref[...])
    q = _dot(hn, wq_ref[...])
    k = _dot(hn, wk_ref[...])
    v = _dot(hn, wv_ref[...])
    return q, k, v, lf


def _attn_proj_sample_kernel(x_ref, g_ref, wq_ref, wk_ref, wv_ref, wf_ref, bf_ref,
                             q_ref, k_ref, v_ref, lf_ref):
    q, k, v, lf = _attn_proj_core(x_ref, g_ref, wq_ref, wk_ref, wv_ref, wf_ref, bf_ref)
    q_ref[...] = q
    k_ref[...] = k
    v_ref[...] = v
    lf_ref[...] = lf[:, :N_HEADS]


def _attn_proj_prompt_kernel(x_ref, g_ref, wq_ref, wk_ref, wv_ref, wf_ref, bf_ref,
                             kt_ref, vt32_ref, lf_ref, qa_ref, ka_ref, vt_ref, cb_ref,
                             carry_ref, *, tiles_per_batch):
    i = pl.program_id(0)
    tm = x_ref.shape[0]
    hn = _rms(x_ref[...], g_ref[...]).astype(BF16)
    lf = _log_sigmoid(_dot(hn, wf_ref[...]) + bf_ref[...])
    lf_ref[...] = lf[:, :N_HEADS]

    lane = lax.broadcasted_iota(jnp.int32, (BLK, LANES), 1)
    lf = jnp.where(lax.broadcasted_iota(jnp.int32, lf.shape, 1) < N_HEADS, lf * LOG2E, 0.0)
    tril = (lax.broadcasted_iota(jnp.int32, (BLK, BLK), 0)
            >= lax.broadcasted_iota(jnp.int32, (BLK, BLK), 1)).astype(BF16)
    scale = HEAD_DIM ** -0.5 * LOG2E

    first = i % tiles_per_batch == 0
    base = jnp.where(first, 0.0, carry_ref[0:1, :])
    cb_ref[...] = jnp.zeros_like(cb_ref)
    cb_ref[0, 0:1, :] = base
    off = jnp.zeros((1, LANES), F32)
    gate_lanes = []
    for jb in range(tm // BLK):
        hi, mid, lo = _split3(lf[jb * BLK:(jb + 1) * BLK])
        floc = _dot(tril, hi) + _dot(tril, mid) + _dot(tril, lo) + off
        off = floc[BLK - 1:BLK, :]
        f_hi = floc.astype(BF16).astype(F32)
        f_lo = floc - f_hi
        fq_odd = f_hi + pltpu.roll(f_lo, N_HEADS, axis=1)
        fk_odd = pltpu.roll(-fq_odd, 32, axis=1)
        gate_lanes.append((pltpu.roll(fq_odd, 64, axis=1), fq_odd, pltpu.roll(fk_odd, 64, axis=1), fk_odd))
    carry_ref[0:1, :] = base + off

    low = lane < 64
    for cg in range(x_ref.shape[1] // (2 * LANES)):
        gcols = slice(cg * 2 * LANES, (cg + 1) * 2 * LANES)
        q = _dot(hn, wq_ref[:, gcols]) * scale
        k = _dot(hn, wk_ref[:, gcols])
        v = _dot(hn, wv_ref[:, gcols])
        for jb in range(tm // BLK):
            rows = slice(jb * BLK, (jb + 1) * BLK)
            fq_even, fq_odd, fk_even, fk_odd = gate_lanes[jb]
            for pp in range(2):
                qc = q[rows, pp * LANES:(pp + 1) * LANES]
                kc = k[rows, pp * LANES:(pp + 1) * LANES]
                he = 4 * cg + 2 * pp
                ho = he + 1
                one_q_even = (lane == 96 + he) | (lane == 112 + he)
                one_q_odd = (lane == 32 + ho) | (lane == 48 + ho)
                one_k_even = (lane == 64 + he) | (lane == 80 + he)
                one_k_odd = (lane == ho) | (lane == 16 + ho)
                qa_e = jnp.where(low, qc, jnp.where(one_q_even, 1.0, fq_even))
                qa_o = jnp.where(low, jnp.where(one_q_odd, 1.0, fq_odd), qc)
                ka_e = jnp.where(low, kc, jnp.where(one_k_even, 1.0, fk_even))
                ka_o = jnp.where(low, jnp.where(one_k_odd, 1.0, fk_odd), kc)
                qa_ref[rows, he * LANES:(he + 1) * LANES] = qa_e.astype(BF16)
                qa_ref[rows, ho * LANES:(ho + 1) * LANES] = qa_o.astype(BF16)
                ka_ref[rows, he * LANES:(he + 1) * LANES] = ka_e.astype(BF16)
                ka_ref[rows, ho * LANES:(ho + 1) * LANES] = ka_o.astype(BF16)
        kt_ref[gcols, :] = k.T
        vt = v.T
        vt32_ref[gcols, :] = vt
        vt_ref[gcols, :] = vt.astype(BF16)


def _attn_proj(x, g, wq, wk, wv, wf, bf, tm, tiles_per_batch=None, l_real=None):
    rows, d = x.shape
    row_spec = pl.BlockSpec((tm, d), lambda i: (i, 0))
    lf_spec = pl.BlockSpec((tm, N_HEADS), lambda i: (i, 0))
    in_specs = [row_spec, _const_spec((1, d)), _const_spec((d, d)), _const_spec((d, d)),
                _const_spec((d, d)), _const_spec((d, LANES)), _const_spec((1, LANES))]
    f32_rows = jax.ShapeDtypeStruct((rows, d), F32)
    lf_shape = jax.ShapeDtypeStruct((rows, N_HEADS), F32)
    if tiles_per_batch is None:
        return pl.pallas_call(
            _attn_proj_sample_kernel,
            grid=(rows // tm,),
            in_specs=in_specs,
            out_specs=[row_spec, row_spec, row_spec, lf_spec],
            out_shape=[f32_rows, f32_rows, f32_rows, lf_shape],
            compiler_params=_params(("parallel",)),
            name="attn_proj_sample",
        )(x, g, wq, wk, wv, wf, bf)
    assert tm == CHUNK
    aug_spec = pl.BlockSpec((tm, 2 * d), lambda i: (i, 0))
    n_tiles = rows // tm
    n_batch = n_tiles // tiles_per_batch
    vt_spec = pl.BlockSpec((None, d, tm), lambda i: (i // tiles_per_batch, 0, i % tiles_per_batch))
    t_shape = jax.ShapeDtypeStruct((n_batch, d, l_real), F32)
    return pl.pallas_call(
        functools.partial(_attn_proj_prompt_kernel, tiles_per_batch=tiles_per_batch),
        grid=(n_tiles,),
        in_specs=in_specs,
        out_specs=[vt_spec, vt_spec, lf_spec, aug_spec, aug_spec, vt_spec,
                   pl.BlockSpec((1, SUBLANES, LANES), lambda i: (i, 0, 0))],
        out_shape=[t_shape, t_shape, lf_shape,
                   jax.ShapeDtypeStruct((rows, 2 * d), BF16),
                   jax.ShapeDtypeStruct((rows, 2 * d), BF16),
                   jax.ShapeDtypeStruct((n_batch, d, tiles_per_batch * tm), BF16),
                   jax.ShapeDtypeStruct((n_tiles, SUBLANES, LANES), F32)],
        scratch_shapes=[pltpu.VMEM((SUBLANES, LANES), F32)],
        compiler_params=_params(("arbitrary",)),
        name="attn_proj_prompt",
    )(x, g, wq, wk, wv, wf, bf)


def _prompt_attn_kernel(cb_ref, qa_ref, ka_ref, vt_ref, o_ref, p_ref, acc_ref, st_ref):
    b = pl.program_id(0)
    hg = pl.program_id(1)
    i = pl.program_id(2)
    nt = pl.num_programs(2)
    hs = p_ref.shape[0]
    rows = [(b * N_HEADS + hg * hs + h) * nt for h in range(hs)]
    ones_rows = jnp.where(lax.broadcasted_iota(jnp.int32, (V_ROWS - HEAD_DIM, CHUNK), 0) == 0,
                          1.0, 0.0).astype(BF16)

    def scores(c, h):
        kj = ka_ref[pl.ds(pl.multiple_of(c * CHUNK, CHUNK), CHUNK), h * LANES:(h + 1) * LANES]
        return _dot_nt(kj, qa_ref[:, h * LANES:(h + 1) * LANES])

    def update_acc(c, h):
        vt = vt_ref[h * HEAD_DIM:(h + 1) * HEAD_DIM, pl.ds(pl.multiple_of(c * CHUNK, CHUNK), CHUNK)]
        pv = _dot(jnp.concatenate([vt, ones_rows], axis=0), p_ref[h])
        acc_ref[h] = st_ref[hs + h:hs + h + 1, :] * acc_ref[h] + pv

    causal = (lax.broadcasted_iota(jnp.int32, (CHUNK, CHUNK), 0)
              <= lax.broadcasted_iota(jnp.int32, (CHUNK, CHUNK), 1))
    for h in range(hs):
        s = jnp.where(causal, scores(i, h), NEG)
        m = jnp.max(s, axis=0, keepdims=True)
        p_ref[h] = jnp.exp2(s - m).astype(BF16)
        st_ref[h:h + 1, :] = m
        st_ref[hs + h:hs + h + 1, :] = jnp.ones((1, CHUNK), F32)
    acc_ref[...] = jnp.zeros_like(acc_ref)

    def step(c, carry):
        pending = jnp.where(c == 0, i, c - 1)
        for h in range(hs):
            s = scores(c, h)
            update_acc(pending, h)
            d = cb_ref[rows[h] + i] - cb_ref[rows[h] + c]
            m = st_ref[h:h + 1, :]
            mn = jnp.maximum(m, jnp.max(s, axis=0, keepdims=True) + d)
            p_ref[h] = jnp.exp2(s - (mn - d)).astype(BF16)
            st_ref[h:h + 1, :] = mn
            st_ref[hs + h:hs + h + 1, :] = jnp.exp2(m - mn)
        return carry

    lax.fori_loop(0, i, step, 0)
    pending = jnp.maximum(i - 1, 0)
    out_t = []
    for h in range(hs):
        update_acc(pending, h)
        out_t.append(acc_ref[h, 0:HEAD_DIM, :] * (1.0 / acc_ref[h, HEAD_DIM:HEAD_DIM + 1, :]))
    o_ref[...] = jnp.concatenate(out_t, axis=0).T.astype(BF16)


def _prompt_attn(cb, qa, ka, vt, n_batch, lp):
    nt = lp // CHUNK
    d = vt.shape[1]
    hs = HEADS_PER_STEP
    ka3 = ka.reshape(n_batch, lp, 2 * d)
    return pl.pallas_call(
        _prompt_attn_kernel,
        grid=(n_batch, N_HEADS // hs, nt),
        in_specs=[pl.BlockSpec(memory_space=pltpu.SMEM),
                  pl.BlockSpec((CHUNK, hs * LANES), lambda b, hg, i: (b * nt + i, hg)),
                  pl.BlockSpec((None, lp, hs * LANES), lambda b, hg, i: (b, 0, hg),
                               pipeline_mode=pl.Buffered(1)),
                  pl.BlockSpec((None, hs * HEAD_DIM, lp), lambda b, hg, i: (b, hg, 0),
                               pipeline_mode=pl.Buffered(1))],
        out_specs=pl.BlockSpec((CHUNK, hs * HEAD_DIM), lambda b, hg, i: (b * nt + i, hg)),
        out_shape=jax.ShapeDtypeStruct((n_batch * lp, d), BF16),
        scratch_shapes=[pltpu.VMEM((hs, CHUNK, CHUNK), BF16),
                        pltpu.VMEM((hs, V_ROWS, CHUNK), F32),
                        pltpu.VMEM((2 * hs, CHUNK), F32)],
        compiler_params=_params(("parallel", "parallel", "parallel")),
        name="prompt_attn",
    )(cb, qa, ka3, vt)


def _decode_attn_kernel(pt_ref, q_ref, kn_ref, vn_ref, lfn_ref, *rest, n_pages_step, n_new):
    np_ = n_pages_step
    k_refs = rest[0:np_]
    v_refs = rest[np_:2 * np_]
    lf_refs = rest[2 * np_:3 * np_]
    o_ref = rest[3 * np_]
    qbd_ref, fc_ref, ct_ref, m_ref, l_ref, acc_ref, kpad_ref, vpad_ref = rest[3 * np_ + 1:]
    s_idx = pl.program_id(1)
    rows = n_new * N_HEADS
    d = q_ref.shape[-1]
    lane = lax.broadcasted_iota(jnp.int32, (rows, LANES), 1)
    row = lax.broadcasted_iota(jnp.int32, (rows, LANES), 0)
    t_of_row = _div_pow2(row, N_HEADS)

    def tile_rows(x):
        return jnp.concatenate([x] * n_new, axis=0)

    def own_head():
        head_of_lane = _div_pow2(lax.broadcasted_iota(jnp.int32, (rows, d), 1), HEAD_DIM)
        head_of_row = _mod_pow2(lax.broadcasted_iota(jnp.int32, (rows, d), 0), N_HEADS)
        return head_of_lane == head_of_row

    @pl.when(s_idx == 0)
    def _():
        q = q_ref[...] * (HEAD_DIM ** -0.5)
        qb = jnp.concatenate([jnp.broadcast_to(q[t:t + 1, :], (N_HEADS, d)) for t in range(n_new)], axis=0)
        qbd = jnp.where(own_head(), qb, 0.0).astype(BF16)
        qbd_ref[...] = qbd
        kpad_ref[...] = jnp.zeros_like(kpad_ref)
        vpad_ref[...] = jnp.zeros_like(vpad_ref)
        kpad_ref[0:n_new, :] = kn_ref[...]
        vpad_ref[0:n_new, :] = vn_ref[...]
        lfn = lfn_ref[...]
        lane_h = lax.broadcasted_iota(jnp.int32, (N_HEADS, LANES), 1)
        run = jnp.zeros((N_HEADS, 1), F32)
        fnt = jnp.zeros((N_HEADS, LANES), F32)
        cols = []
        for t in range(n_new):
            run = run + lfn[:, t:t + 1]
            cols.append(run)
            fnt = jnp.where(lane_h == t, run, fnt)
        fn_col = jnp.concatenate(cols, axis=0)
        fc_ref[...] = fn_col
        s = _dot_nt(qbd, kpad_ref[...].astype(BF16))
        s = s + fn_col - tile_rows(fnt)
        s = jnp.where(lane <= t_of_row, s, NEG)
        m = jnp.max(s, axis=-1, keepdims=True)
        p = jnp.exp(s - m)
        m_ref[...] = m
        l_ref[...] = jnp.sum(p, axis=-1, keepdims=True)
        acc_ref[...] = _dot(p.astype(BF16), vpad_ref[...].astype(BF16))
        ct_ref[...] = jnp.zeros_like(ct_ref)

    qbd = qbd_ref[...]
    strict_lower = (lax.broadcasted_iota(jnp.int32, (LANES, LANES), 0)
                    > lax.broadcasted_iota(jnp.int32, (LANES, LANES), 1)).astype(BF16)
    ct = ct_ref[...]
    fn_col = fc_ref[...]
    scores = []
    for p_i in range(np_):
        lft = lf_refs[p_i][...]
        hi, mid, lo = _split3(lft)
        gloc = _dot(hi, strict_lower) + _dot(mid, strict_lower) + _dot(lo, strict_lower)
        s = _dot(qbd, k_refs[p_i][...].astype(BF16))
        scores.append(s + tile_rows(gloc) + (fn_col + tile_rows(ct)))
        ct = ct + jnp.sum(lft, axis=-1, keepdims=True)
    ct_ref[...] = ct
    m_old = m_ref[...]
    m_new = m_old
    for s in scores:
        m_new = jnp.maximum(m_new, jnp.max(s, axis=-1, keepdims=True))
    alpha = jnp.exp(m_old - m_new)
    l_new = alpha * l_ref[...]
    acc = alpha * acc_ref[...]
    for p_i in range(np_):
        p = jnp.exp(scores[p_i] - m_new)
        l_new = l_new + jnp.sum(p, axis=-1, keepdims=True)
        acc = acc + _dot_nt(p.astype(BF16), v_refs[p_i][...].astype(BF16))
    m_ref[...] = m_new
    l_ref[...] = l_new
    acc_ref[...] = acc

    @pl.when(s_idx == pl.num_programs(1) - 1)
    def _():
        o = jnp.where(own_head(), acc * (1.0 / l_new), 0.0)
        for t in range(n_new):
            o_ref[t:t + 1, :] = jnp.sum(o[t * N_HEADS:(t + 1) * N_HEADS, :], axis=0, keepdims=True)


def _decode_attn(page_table, q, kn, vn, lfn_t, cache_kt, cache_vt, cache_lf_t, n_pages_step):
    db, n_new, d = q.shape
    n_pages = page_table.shape[1]
    page = cache_kt.shape[2]
    np_ = n_pages_step
    assert n_pages % np_ == 0 and page == LANES
    steps = n_pages // np_
    rows = n_new * N_HEADS

    def page_map(p_i):
        return lambda b, s, pt: (pt[b, n_pages - 1 - (s * np_ + p_i)], 0, 0)

    new_spec = pl.BlockSpec((None, n_new, d), lambda b, s, pt: (b, 0, 0))
    in_specs = ([new_spec, new_spec, new_spec,
                 pl.BlockSpec((None, N_HEADS, LANES), lambda b, s, pt: (b, 0, 0))]
                + [pl.BlockSpec((None, d, page), page_map(p_i)) for p_i in range(np_)]
                + [pl.BlockSpec((None, d, page), page_map(p_i)) for p_i in range(np_)]
                + [pl.BlockSpec((None, N_HEADS, page), page_map(p_i)) for p_i in range(np_)])
    grid_spec = pltpu.PrefetchScalarGridSpec(
        num_scalar_prefetch=1,
        grid=(db, steps),
        in_specs=in_specs,
        out_specs=pl.BlockSpec((None, n_new, d), lambda b, s, pt: (b, 0, 0)),
        scratch_shapes=[pltpu.VMEM((rows, d), BF16),
                        pltpu.VMEM((rows, 1), F32),
                        pltpu.VMEM((N_HEADS, 1), F32),
                        pltpu.VMEM((rows, 1), F32),
                        pltpu.VMEM((rows, 1), F32),
                        pltpu.VMEM((rows, d), F32),
                        pltpu.VMEM((LANES, d), F32),
                        pltpu.VMEM((LANES, d), F32)])
    return pl.pallas_call(
        functools.partial(_decode_attn_kernel, n_pages_step=np_, n_new=n_new),
        grid_spec=grid_spec,
        out_shape=jax.ShapeDtypeStruct((db, n_new, d), F32),
        compiler_params=_params(("parallel", "arbitrary")),
        name="decode_attn",
    )(page_table, q, kn, vn, lfn_t, *([cache_kt] * np_), *([cache_vt] * np_), *([cache_lf_t] * np_))


def _conv_core(x_ref, gpre_ref, wb_ref, wc_ref, wh_ref):
    hn = _rms(x_ref[...], gpre_ref[...]).astype(BF16)
    return _dot(hn, wb_ref[...]), _dot(hn, wc_ref[...]) * _dot(hn, wh_ref[...])


def _conv_finish(x_ref, gate, conv, wo_ref, gpost_ref, y_ref):
    m = _dot((gate * conv).astype(BF16), wo_ref[...])
    y_ref[...] = x_ref[...] + _rms(m, gpost_ref[...])


def _conv_prompt_kernel(x_ref, gpre_ref, wb_ref, wc_ref, wh_ref, ck_ref, wo_ref, gpost_ref,
                        y_ref, tail_ref, u_ref, gc_ref, *, tiles_per_batch, pad_rows):
    i = pl.program_id(0)
    tm = x_ref.shape[0]
    hist = SUBLANES

    @pl.when(i == 0)
    def _():
        u_ref[0:hist, :] = jnp.zeros((hist, u_ref.shape[1]), F32)

    @pl.when(i > 0)
    def _():
        u_ref[0:hist, :] = u_ref[tm:tm + hist, :]

    hn = _rms(x_ref[...], gpre_ref[...]).astype(BF16)
    last = i % tiles_per_batch == tiles_per_batch - 1
    n_real = jnp.where(last, tm - pad_rows, tm)
    real = lax.broadcasted_iota(jnp.int32, (tm, 1), 0) < n_real
    gw = 2 * LANES
    for cg in range(x_ref.shape[1] // gw):
        cols = slice(cg * gw, (cg + 1) * gw)
        gate = _dot(hn, wb_ref[:, cols])
        u = jnp.where(real, _dot(hn, wc_ref[:, cols]) * _dot(hn, wh_ref[:, cols]), 0.0)
        u_ref[hist:hist + tm, cols] = u
        tail_ref[0, :, cols] = u[tm - pad_rows - hist:tm - pad_rows, :]
        ck = ck_ref[:, cols]
        conv = (ck[0:1, :] * u_ref[hist - 2:hist - 2 + tm, cols]
                + ck[1:2, :] * u_ref[hist - 1:hist - 1 + tm, cols]
                + ck[2:3, :] * u)
        gc_ref[:, cols] = (gate * conv).astype(BF16)
    y_ref[...] = x_ref[...] + _rms(_dot(gc_ref[...], wo_ref[...]), gpost_ref[...])


def _conv_sample_kernel(x_ref, gpre_ref, wb_ref, wc_ref, wh_ref, ck_ref, wo_ref, gpost_ref,
                        h1_ref, h2_ref, y_ref, uo_ref, u_ref, *, n_new):
    tm = x_ref.shape[0]
    hist = SUBLANES
    gate, u = _conv_core(x_ref, gpre_ref, wb_ref, wc_ref, wh_ref)
    uo_ref[...] = u
    u_ref[0:hist, :] = jnp.zeros((hist, u_ref.shape[1]), F32)
    u_ref[hist:hist + tm, :] = u
    t = _mod_pow2(lax.broadcasted_iota(jnp.int32, (tm, 1), 0), n_new)
    ck = ck_ref[...]
    conv = (ck[0:1, :] * jnp.where(t >= 2, u_ref[hist - 2:hist - 2 + tm, :], h2_ref[...])
            + ck[1:2, :] * jnp.where(t >= 1, u_ref[hist - 1:hist - 1 + tm, :], h1_ref[...])
            + ck[2:3, :] * u)
    _conv_finish(x_ref, gate, conv, wo_ref, gpost_ref, y_ref)


def _conv_mixer(x, g_pre, wb, wc, wh, ck, wo, g_post, tm, tiles_per_batch=None, pad_rows=None, hist1=None,
                hist2=None, n_new=None):
    rows, d = x.shape
    row_spec = pl.BlockSpec((tm, d), lambda i: (i, 0))
    w_spec = _const_spec((d, d))
    in_specs = [row_spec, _const_spec((1, d)), w_spec, w_spec, w_spec, _const_spec((CONV_W, d)),
                w_spec, _const_spec((1, d))]
    f32_rows = jax.ShapeDtypeStruct((rows, d), F32)
    scratch = [pltpu.VMEM((tm + 2 * SUBLANES, d), F32)]
    if tiles_per_batch is None:
        return pl.pallas_call(
            functools.partial(_conv_sample_kernel, n_new=n_new),
            grid=(rows // tm,),
            in_specs=in_specs + [row_spec, row_spec],
            out_specs=[row_spec, row_spec],
            out_shape=[f32_rows, f32_rows],
            scratch_shapes=scratch,
            compiler_params=_params(("parallel",)),
            name="conv_mixer_sample",
        )(x, g_pre, wb, wc, wh, ck, wo, g_post, hist1, hist2)
    n_tiles = rows // tm
    assert pad_rows % SUBLANES == 0 and SUBLANES <= tm - pad_rows
    return pl.pallas_call(
        functools.partial(_conv_prompt_kernel, tiles_per_batch=tiles_per_batch, pad_rows=pad_rows),
        grid=(n_tiles,),
        in_specs=in_specs,
        out_specs=[row_spec, pl.BlockSpec((1, SUBLANES, d), lambda i: (i, 0, 0))],
        out_shape=[f32_rows, jax.ShapeDtypeStruct((n_tiles, SUBLANES, d), F32)],
        scratch_shapes=scratch + [pltpu.VMEM((tm, d), BF16)],
        compiler_params=_params(("arbitrary",)),
        name="conv_mixer_prompt",
    )(x, g_pre, wb, wc, wh, ck, wo, g_post)


def kernel(x_prompt, x_sample, cache_k, cache_v, cache_logf, state_conv, page_table, meta_tokens, norm_g,
           ffn_w_gate, ffn_w_up, ffn_w_down, attn_w_in, attn_b_f, attn_w_out, conv_w_in, conv_kernel,
           conv_w_out):
    n_batch, seq, d = x_prompt.shape
    db, n_new, _ = x_sample.shape
    depth = norm_g.shape[0]
    n_pool, page = cache_k.shape[1], cache_k.shape[2]
    l_real = N_META + seq
    pad_rows = (-l_real) % CHUNK
    lp = l_real + pad_rows
    tpb = lp // CHUNK
    tm_p = CHUNK
    tm_r = ROW_TILE if (n_batch * lp) % ROW_TILE == 0 else CHUNK
    tm_s = db * n_new
    assert d == D_MODEL and page == LANES and tm_s % SUBLANES == 0

    meta = jnp.broadcast_to(meta_tokens[None].astype(F32), (n_batch, N_META, d))
    xp = jnp.concatenate([meta, x_prompt, jnp.zeros((n_batch, pad_rows, d), F32)], axis=1).reshape(n_batch * lp, d)
    xs = x_sample.reshape(tm_s, d)

    def row(v):
        return v.reshape(1, -1).astype(F32)

    k_p, v_p, lf_p, cv_p, k_s, v_s, lf_s, cv_s = [], [], [], [], [], [], [], []
    for i in range(depth):
        g = norm_g[i]
        wg, wu, wd = (w[i].astype(BF16) for w in (ffn_w_gate, ffn_w_up, ffn_w_down))
        xp = _ffn(xp, row(g[0]), row(g[1]), wg[0], wu[0], wd[0], tm_r)
        xs = _ffn(xs, row(g[0]), row(g[1]), wg[0], wu[0], wd[0], tm_s)
        if i % 2 == 0:
            a = i // 2
            w_in = attn_w_in[a]
            wq, wk, wv = (w_in[:, j * d:(j + 1) * d].astype(BF16) for j in range(3))
            wf = jnp.pad(w_in[:, 3 * d:], ((0, 0), (0, LANES - N_HEADS))).astype(BF16)
            bf = jnp.pad(attn_b_f[a], (0, LANES - N_HEADS)).reshape(1, LANES).astype(F32)
            wo = attn_w_out[a].astype(BF16)

            kt, vt32, lfp, qa, ka, vt, cb = _attn_proj(xp, row(g[2]), wq, wk, wv, wf, bf, tm_p, tpb, l_real)
            cb = cb[:, 0, :N_HEADS].reshape(n_batch, tpb, N_HEADS)
            cb = cb.transpose(0, 2, 1).reshape(n_batch * N_HEADS * tpb)
            op = _prompt_attn(cb, qa, ka, vt, n_batch, lp)
            mix_p = (op, wo, row(g[3]))

            qs, ks, vs, lfs = _attn_proj(xs, row(g[2]), wq, wk, wv, wf, bf, tm_s)
            lfn_t = jnp.pad(lfs.reshape(db, n_new, N_HEADS).transpose(0, 2, 1),
                            ((0, 0), (0, 0), (0, LANES - n_new)))
            ck_t = cache_k[a].transpose(0, 2, 3, 1).reshape(n_pool, d, page)
            cv_t = cache_v[a].transpose(0, 2, 3, 1).reshape(n_pool, d, page)
            clf_t = cache_logf[a].astype(F32).transpose(0, 2, 1)
            osm = _decode_attn(page_table, qs.reshape(db, n_new, d), ks.reshape(db, n_new, d),
                               vs.reshape(db, n_new, d), lfn_t, ck_t, cv_t, clf_t, DECODE_PAGES_PER_STEP)
            mix_s = (osm.reshape(tm_s, d), wo, row(g[3]))

            k_p.append(kt.reshape(n_batch, N_HEADS, HEAD_DIM, l_real).transpose(0, 3, 1, 2))
            v_p.append(vt32.reshape(n_batch, N_HEADS, HEAD_DIM, l_real).transpose(0, 3, 1, 2))
            lf_p.append(lfp.reshape(n_batch, lp, N_HEADS)[:, :l_real])
            k_s.append(ks.reshape(db, n_new, N_HEADS, HEAD_DIM))
            v_s.append(vs.reshape(db, n_new, N_HEADS, HEAD_DIM))
            lf_s.append(lfs.reshape(db, n_new, N_HEADS))
        else:
            c = i // 2
            w_in = conv_w_in[c]
            wb, wc, wh = (w_in[:, j * d:(j + 1) * d].astype(BF16) for j in range(3))
            wo = conv_w_out[c].astype(BF16)
            ck = conv_kernel[c].astype(F32)
            xp, tails = _conv_mixer(xp, row(g[2]), wb, wc, wh, ck, wo, row(g[3]), tm_p, tiles_per_batch=tpb,
                                    pad_rows=pad_rows)
            tails = tails.reshape(n_batch, tpb, SUBLANES, d)
            cv_p.append(tails[:, tpb - 1, SUBLANES - (CONV_W - 1):])

            st = state_conv[c].astype(F32)
            zero = jnp.zeros((db, n_new, d), F32)
            hist1 = zero.at[:, 0].set(st[:, 1]).reshape(tm_s, d)
            hist2 = zero.at[:, 0].set(st[:, 0]).at[:, 1].set(st[:, 1]).reshape(tm_s, d)
            xs, us = _conv_mixer(xs, row(g[2]), wb, wc, wh, ck, wo, row(g[3]), tm_s, hist1=hist1,
                                 hist2=hist2, n_new=n_new)
            us_ext = jnp.concatenate([st, us.reshape(db, n_new, d)], axis=1)
            cv_s.append(us_ext[:, -(CONV_W - 1):])
            mix_p = mix_s = None
        final = i == depth - 1 and mix_p is None
        if final:
            tm_y = next(t for t in (ROW_TILE, CHUNK, BLK, SUBLANES) if seq % t == 0)
            y_prompt = _ffn(xp, row(g[4]), row(g[5]), wg[1], wu[1], wd[1], tm_y, window=(n_batch, N_META, seq))
        else:
            xp = _ffn(xp, row(g[4]), row(g[5]), wg[1], wu[1], wd[1], tm_r, mix_p)
        xs = _ffn(xs, row(g[4]), row(g[5]), wg[1], wu[1], wd[1], tm_s, mix_s)

    if not final:
        y_prompt = xp.reshape(n_batch, lp, d)[:, N_META:l_real]
    y_sample = xs.reshape(db, n_new, d)
    return (y_prompt, y_sample, jnp.stack(k_p), jnp.stack(v_p), jnp.stack(lf_p), jnp.stack(cv_p),
            jnp.stack(k_s), jnp.stack(v_s), jnp.stack(lf_s), jnp.stack(cv_s))
```

```python
import functools

import jax
import jax.numpy as jnp
from jax import lax
from jax.experimental import pallas as pl
from jax.experimental.pallas import tpu as pltpu

D_MODEL = 1024
N_HEADS = 16
HEAD_DIM = 64
N_META = 16
CONV_W = 3
NORM_EPS = 1e-6
HALF = 0.5

LANES = 128
SUBLANES = 8
BLK = 128
CHUNK = 2 * BLK
ROW_TILE = 4 * CHUNK
V_ROWS = HEAD_DIM + 16
HEADS_PER_STEP = 16
DECODE_PAGES_PER_STEP = 16
FF_CHUNK = 256
VMEM_LIMIT = 56 * 1024 * 1024
NEG = -1e30
LOG2E = 1.4426950408889634

F32 = jnp.float32
BF16 = jnp.bfloat16


def _dot(a, b):
    return jnp.dot(a, b, preferred_element_type=F32)


def _dot_nt(a, b):
    return lax.dot_general(a, b, (((1,), (1,)), ((), ())), preferred_element_type=F32)


def _div_pow2(x, n):
    assert n & (n - 1) == 0
    return lax.shift_right_logical(x, n.bit_length() - 1)


def _mod_pow2(x, n):
    assert n & (n - 1) == 0
    return x & (n - 1)


def _rms(x, g):
    ms = jnp.mean(x * x, axis=-1, keepdims=True)
    return x * lax.rsqrt(ms + NORM_EPS) * g


def _split3(x):
    hi = x.astype(BF16)
    r = x - hi.astype(F32)
    mid = r.astype(BF16)
    lo = (r - mid.astype(F32)).astype(BF16)
    return hi, mid, lo


def _const_spec(shape):
    return pl.BlockSpec(shape, lambda *_: (0,) * len(shape), pipeline_mode=pl.Buffered(1))


def _params(sem):
    return pltpu.CompilerParams(dimension_semantics=sem, vmem_limit_bytes=VMEM_LIMIT)


def _ffn_kernel(*refs, with_mixer):
    if with_mixer:
        x_ref, mix_ref, wmix_ref, gmix_ref = refs[:4]
        refs = refs[:1] + refs[4:]
    x_ref, gpre_ref, gpost_ref, wg_ref, wu_ref, wd_ref, o_ref, h_ref = refs
    x = x_ref[...]
    if with_mixer:
        x = x + _rms(_dot(mix_ref[...].astype(BF16), wmix_ref[...]), gmix_ref[...])
    xn = _rms(x, gpre_ref[...]).astype(BF16)
    d_ff = wg_ref.shape[1]
    for c in range(d_ff // FF_CHUNK):
        sl = slice(c * FF_CHUNK, (c + 1) * FF_CHUNK)
        g = _dot(xn, wg_ref[:, sl])
        u = _dot(xn, wu_ref[:, sl])
        h_ref[:, sl] = (g * (1.0 / (1.0 + jnp.exp(-g))) * u).astype(BF16)
    y = _dot(h_ref[...], wd_ref[...])
    o_ref[...] = x + HALF * _rms(y, gpost_ref[...])


def _ffn(x, g_pre, g_post, wg, wu, wd, tm, mixer=None, window=None):
    rows, d = x.shape
    d_ff = wg.shape[1]
    assert d_ff % FF_CHUNK == 0
    w_specs = [_const_spec((1, d)), _const_spec((1, d)), _const_spec((d, d_ff)), _const_spec((d, d_ff)),
               _const_spec((d_ff, d))]
    if window is None:
        assert rows % tm == 0
        grid = (rows // tm,)
        row_spec = out_spec = pl.BlockSpec((tm, d), lambda i: (i, 0))
        out_shape = jax.ShapeDtypeStruct((rows, d), F32)
    else:
        n_batch, row0, n_rows = window
        assert mixer is None and n_rows % tm == 0 and row0 % SUBLANES == 0
        group = rows // n_batch
        grid = (n_batch, n_rows // tm)
        assert group % SUBLANES == 0 and tm % SUBLANES == 0
        row_spec = pl.BlockSpec((pl.Element(tm), pl.Element(d)),
                                lambda b, t: (pl.multiple_of(b * group + row0 + t * tm, SUBLANES), 0))
        out_spec = pl.BlockSpec((None, tm, d), lambda b, t: (b, t, 0))
        out_shape = jax.ShapeDtypeStruct((n_batch, n_rows, d), F32)
    mixer_specs = [] if mixer is None else [row_spec, _const_spec((d, d)), _const_spec((1, d))]
    return pl.pallas_call(
        functools.partial(_ffn_kernel, with_mixer=mixer is not None),
        grid=grid,
        in_specs=[row_spec] + mixer_specs + w_specs,
        out_specs=out_spec,
        out_shape=out_shape,
        scratch_shapes=[pltpu.VMEM((tm, d_ff), BF16)],
        compiler_params=_params(("parallel",) * len(grid)),
        name="half_ffn",
    )(x, *(mixer or ()), g_pre, g_post, wg, wu, wd)


def _log_sigmoid(z):
    return jnp.minimum(z, 0.0) - jnp.log1p(jnp.exp(-jnp.abs(z)))


def _attn_proj_core(x_ref, g_ref, wq_ref, wk_ref, wv_ref, wf_ref, bf_ref):
    hn = _rms(x_ref[...], g_ref[...]).astype(BF16)
    lf = _log_sigmoid(_dot(hn, wf_ref[...]) + bf_ref[...])
    q = _dot(hn, wq_ref[...])
    k = _dot(hn, wk_ref[...])
    v = _dot(hn, wv_ref[...])
    return q, k, v, lf


def _attn_proj_sample_kernel(x_ref, g_ref, wq_ref, wk_ref, wv_ref, wf_ref, bf_ref,
                             q_ref, k_ref, v_ref, lf_ref):
    q, k, v, lf = _attn_proj_core(x_ref, g_ref, wq_ref, wk_ref, wv_ref, wf_ref, bf_ref)
    q_ref[...] = q
    k_ref[...] = k
    v_ref[...] = v
    lf_ref[...] = lf[:, :N_HEADS]


def _attn_proj_prompt_kernel(x_ref, g_ref, wq_ref, wk_ref, wv_ref, wf_ref, bf_ref,
                             kt_ref, vt32_ref, lf_ref, qa_ref, ka_ref, vt_ref, cb_ref,
                             carry_ref, *, tiles_per_batch):
    i = pl.program_id(0)
    tm = x_ref.shape[0]
    hn = _rms(x_ref[...], g_ref[...]).astype(BF16)
    lf = _log_sigmoid(_dot(hn, wf_ref[...]) + bf_ref[...])
    lf_ref[...] = lf[:, :N_HEADS]

    lane = lax.broadcasted_iota(jnp.int32, (BLK, LANES), 1)
    lf = jnp.where(lax.broadcasted_iota(jnp.int32, lf.shape, 1) < N_HEADS, lf * LOG2E, 0.0)
    tril = (lax.broadcasted_iota(jnp.int32, (BLK, BLK), 0)
            >= lax.broadcasted_iota(jnp.int32, (BLK, BLK), 1)).astype(BF16)
    scale = HEAD_DIM ** -0.5 * LOG2E

    first = i % tiles_per_batch == 0
    base = jnp.where(first, 0.0, carry_ref[0:1, :])
    cb_ref[...] = jnp.zeros_like(cb_ref)
    cb_ref[0, 0:1, :] = base
    off = jnp.zeros((1, LANES), F32)
    gate_lanes = []
    for jb in range(tm // BLK):
        hi, mid, lo = _split3(lf[jb * BLK:(jb + 1) * BLK])
        floc = _dot(tril, hi) + _dot(tril, mid) + _dot(tril, lo) + off
        off = floc[BLK - 1:BLK, :]
        f_hi = floc.astype(BF16).astype(F32)
        f_lo = floc - f_hi
        fq_odd = f_hi + pltpu.roll(f_lo, N_HEADS, axis=1)
        fk_odd = pltpu.roll(-fq_odd, 32, axis=1)
        gate_lanes.append((pltpu.roll(fq_odd, 64, axis=1), fq_odd, pltpu.roll(fk_odd, 64, axis=1), fk_odd))
    carry_ref[0:1, :] = base + off

    low = lane < 64
    for cg in range(x_ref.shape[1] // (2 * LANES)):
        gcols = slice(cg * 2 * LANES, (cg + 1) * 2 * LANES)
        q = _dot(hn, wq_ref[:, gcols]) * scale
        k = _dot(hn, wk_ref[:, gcols])
        v = _dot(hn, wv_ref[:, gcols])
        for jb in range(tm // BLK):
            rows = slice(jb * BLK, (jb + 1) * BLK)
            fq_even, fq_odd, fk_even, fk_odd = gate_lanes[jb]
            for pp in range(2):
                qc = q[rows, pp * LANES:(pp + 1) * LANES]
                kc = k[rows, pp * LANES:(pp + 1) * LANES]
                he = 4 * cg + 2 * pp
                ho = he + 1
                one_q_even = (lane == 96 + he) | (lane == 112 + he)
                one_q_odd = (lane == 32 + ho) | (lane == 48 + ho)
                one_k_even = (lane == 64 + he) | (lane == 80 + he)
                one_k_odd = (lane == ho) | (lane == 16 + ho)
                qa_e = jnp.where(low, qc, jnp.where(one_q_even, 1.0, fq_even))
                qa_o = jnp.where(low, jnp.where(one_q_odd, 1.0, fq_odd), qc)
                ka_e = jnp.where(low, kc, jnp.where(one_k_even, 1.0, fk_even))
                ka_o = jnp.where(low, jnp.where(one_k_odd, 1.0, fk_odd), kc)
                qa_ref[rows, he * LANES:(he + 1) * LANES] = qa_e.astype(BF16)
                qa_ref[rows, ho * LANES:(ho + 1) * LANES] = qa_o.astype(BF16)
                ka_ref[rows, he * LANES:(he + 1) * LANES] = ka_e.astype(BF16)
                ka_ref[rows, ho * LANES:(ho + 1) * LANES] = ka_o.astype(BF16)
        kt_ref[gcols, :] = k.T
        vt = v.T
        vt32_ref[gcols, :] = vt
        vt_ref[gcols, :] = vt.astype(BF16)


def _attn_proj(x, g, wq, wk, wv, wf, bf, tm, tiles_per_batch=None, l_real=None):
    rows, d = x.shape
    row_spec = pl.BlockSpec((tm, d), lambda i: (i, 0))
    lf_spec = pl.BlockSpec((tm, N_HEADS), lambda i: (i, 0))
    in_specs = [row_spec, _const_spec((1, d)), _const_spec((d, d)), _const_spec((d, d)),
                _const_spec((d, d)), _const_spec((d, LANES)), _const_spec((1, LANES))]
    f32_rows = jax.ShapeDtypeStruct((rows, d), F32)
    lf_shape = jax.ShapeDtypeStruct((rows, N_HEADS), F32)
    if tiles_per_batch is None:
        return pl.pallas_call(
            _attn_proj_sample_kernel,
            grid=(rows // tm,),
            in_specs=in_specs,
            out_specs=[row_spec, row_spec, row_spec, lf_spec],
            out_shape=[f32_rows, f32_rows, f32_rows, lf_shape],
            compiler_params=_params(("parallel",)),
            name="attn_proj_sample",
        )(x, g, wq, wk, wv, wf, bf)
    assert tm == CHUNK
    aug_spec = pl.BlockSpec((tm, 2 * d), lambda i: (i, 0))
    n_tiles = rows // tm
    n_batch = n_tiles // tiles_per_batch
    vt_spec = pl.BlockSpec((None, d, tm), lambda i: (i // tiles_per_batch, 0, i % tiles_per_batch))
    t_shape = jax.ShapeDtypeStruct((n_batch, d, l_real), F32)
    return pl.pallas_call(
        functools.partial(_attn_proj_prompt_kernel, tiles_per_batch=tiles_per_batch),
        grid=(n_tiles,),
        in_specs=in_specs,
        out_specs=[vt_spec, vt_spec, lf_spec, aug_spec, aug_spec, vt_spec,
                   pl.BlockSpec((1, SUBLANES, LANES), lambda i: (i, 0, 0))],
        out_shape=[t_shape, t_shape, lf_shape,
                   jax.ShapeDtypeStruct((rows, 2 * d), BF16),
                   jax.ShapeDtypeStruct((rows, 2 * d), BF16),
                   jax.ShapeDtypeStruct((n_batch, d, tiles_per_batch * tm), BF16),
                   jax.ShapeDtypeStruct((n_tiles, SUBLANES, LANES), F32)],
        scratch_shapes=[pltpu.VMEM((SUBLANES, LANES), F32)],
        compiler_params=_params(("arbitrary",)),
        name="attn_proj_prompt",
    )(x, g, wq, wk, wv, wf, bf)


def _prompt_attn_kernel(cb_ref, qa_ref, ka_ref, vt_ref, o_ref, p_ref, acc_ref, st_ref, qt_ref):
    b = pl.program_id(0)
    hg = pl.program_id(1)
    i = pl.program_id(2)
    nt = pl.num_programs(2)
    hs = p_ref.shape[0]
    rows = [(b * N_HEADS + hg * hs + h) * nt for h in range(hs)]
    ones_rows = jnp.where(lax.broadcasted_iota(jnp.int32, (V_ROWS - HEAD_DIM, CHUNK), 0) == 0,
                          1.0, 0.0).astype(BF16)

    def scores(c, h):
        kj = ka_ref[pl.ds(pl.multiple_of(c * CHUNK, CHUNK), CHUNK), h * LANES:(h + 1) * LANES]
        return _dot(kj, qt_ref[h])

    def update_acc(c, h):
        vt = vt_ref[h * HEAD_DIM:(h + 1) * HEAD_DIM, pl.ds(pl.multiple_of(c * CHUNK, CHUNK), CHUNK)]
        pv = _dot(jnp.concatenate([vt, ones_rows], axis=0), p_ref[h])
        acc_ref[h] = st_ref[hs + h:hs + h + 1, :] * acc_ref[h] + pv

    for h in range(hs):
        qt_ref[h] = qa_ref[:, h * LANES:(h + 1) * LANES].astype(F32).T.astype(BF16)

    causal = (lax.broadcasted_iota(jnp.int32, (CHUNK, CHUNK), 0)
              <= lax.broadcasted_iota(jnp.int32, (CHUNK, CHUNK), 1))
    for h in range(hs):
        s = jnp.where(causal, scores(i, h), NEG)
        m = jnp.max(s, axis=0, keepdims=True)
        p_ref[h] = jnp.exp2(s - m).astype(BF16)
        st_ref[h:h + 1, :] = m
        st_ref[hs + h:hs + h + 1, :] = jnp.ones((1, CHUNK), F32)
    acc_ref[...] = jnp.zeros_like(acc_ref)

    def step(c, carry):
        pending = jnp.where(c == 0, i, c - 1)
        for h in range(hs):
            s = scores(c, h)
            update_acc(pending, h)
            d = cb_ref[rows[h] + i] - cb_ref[rows[h] + c]
            m = st_ref[h:h + 1, :]
            mn = jnp.maximum(m, jnp.max(s, axis=0, keepdims=True) + d)
            p_ref[h] = jnp.exp2(s - (mn - d)).astype(BF16)
            st_ref[h:h + 1, :] = mn
            st_ref[hs + h:hs + h + 1, :] = jnp.exp2(m - mn)
        return carry

    lax.fori_loop(0, i, step, 0)
    pending = jnp.maximum(i - 1, 0)
    out_t = []
    for h in range(hs):
        update_acc(pending, h)
        out_t.append(acc_ref[h, 0:HEAD_DIM, :] * (1.0 / acc_ref[h, HEAD_DIM:HEAD_DIM + 1, :]))
    o_ref[...] = jnp.concatenate(out_t, axis=0).T.astype(BF16)


def _prompt_attn(cb, qa, ka, vt, n_batch, lp):
    nt = lp // CHUNK
    d = vt.shape[1]
    hs = HEADS_PER_STEP
    ka3 = ka.reshape(n_batch, lp, 2 * d)
    return pl.pallas_call(
        _prompt_attn_kernel,
        grid=(n_batch, N_HEADS // hs, nt),
        in_specs=[pl.BlockSpec(memory_space=pltpu.SMEM),
                  pl.BlockSpec((CHUNK, hs * LANES), lambda b, hg, i: (b * nt + i, hg)),
                  pl.BlockSpec((None, lp, hs * LANES), lambda b, hg, i: (b, 0, hg),
                               pipeline_mode=pl.Buffered(1)),
                  pl.BlockSpec((None, hs * HEAD_DIM, lp), lambda b, hg, i: (b, hg, 0),
                               pipeline_mode=pl.Buffered(1))],
        out_specs=pl.BlockSpec((CHUNK, hs * HEAD_DIM), lambda b, hg, i: (b * nt + i, hg)),
        out_shape=jax.ShapeDtypeStruct((n_batch * lp, d), BF16),
        scratch_shapes=[pltpu.VMEM((hs, CHUNK, CHUNK), BF16),
                        pltpu.VMEM((hs, V_ROWS, CHUNK), F32),
                        pltpu.VMEM((2 * hs, CHUNK), F32),
                        pltpu.VMEM((hs, LANES, CHUNK), BF16)],
        compiler_params=_params(("parallel", "parallel", "parallel")),
        name="prompt_attn",
    )(cb, qa, ka3, vt)


def _decode_attn_kernel(pt_ref, q_ref, kn_ref, vn_ref, lfn_ref, *rest, n_pages_step, n_new):
    np_ = n_pages_step
    k_refs = rest[0:np_]
    v_refs = rest[np_:2 * np_]
    lf_refs = rest[2 * np_:3 * np_]
    o_ref = rest[3 * np_]
    qbd_ref, fc_ref, ct_ref, m_ref, l_ref, acc_ref, kpad_ref, vpad_ref = rest[3 * np_ + 1:]
    s_idx = pl.program_id(1)
    rows = n_new * N_HEADS
    d = q_ref.shape[-1]
    lane = lax.broadcasted_iota(jnp.int32, (rows, LANES), 1)
    row = lax.broadcasted_iota(jnp.int32, (rows, LANES), 0)
    t_of_row = _div_pow2(row, N_HEADS)

    def tile_rows(x):
        return jnp.concatenate([x] * n_new, axis=0)

    def own_head():
        head_of_lane = _div_pow2(lax.broadcasted_iota(jnp.int32, (rows, d), 1), HEAD_DIM)
        head_of_row = _mod_pow2(lax.broadcasted_iota(jnp.int32, (rows, d), 0), N_HEADS)
        return head_of_lane == head_of_row

    @pl.when(s_idx == 0)
    def _():
        q = q_ref[...] * (HEAD_DIM ** -0.5)
        qb = jnp.concatenate([jnp.broadcast_to(q[t:t + 1, :], (N_HEADS, d)) for t in range(n_new)], axis=0)
        qbd = jnp.where(own_head(), qb, 0.0).astype(BF16)
        qbd_ref[...] = qbd
        kpad_ref[...] = jnp.zeros_like(kpad_ref)
        vpad_ref[...] = jnp.zeros_like(vpad_ref)
        kpad_ref[0:n_new, :] = kn_ref[...]
        vpad_ref[0:n_new, :] = vn_ref[...]
        lfn = lfn_ref[...]
        lane_h = lax.broadcasted_iota(jnp.int32, (N_HEADS, LANES), 1)
        run = jnp.zeros((N_HEADS, 1), F32)
        fnt = jnp.zeros((N_HEADS, LANES), F32)
        cols = []
        for t in range(n_new):
            run = run + lfn[:, t:t + 1]
            cols.append(run)
            fnt = jnp.where(lane_h == t, run, fnt)
        fn_col = jnp.concatenate(cols, axis=0)
        fc_ref[...] = fn_col
        s = _dot_nt(qbd, kpad_ref[...].astype(BF16))
        s = s + fn_col - tile_rows(fnt)
        s = jnp.where(lane <= t_of_row, s, NEG)
        m = jnp.max(s, axis=-1, keepdims=True)
        p = jnp.exp(s - m)
        m_ref[...] = m
        l_ref[...] = jnp.sum(p, axis=-1, keepdims=True)
        acc_ref[...] = _dot(p.astype(BF16), vpad_ref[...].astype(BF16))
        ct_ref[...] = jnp.zeros_like(ct_ref)

    qbd = qbd_ref[...]
    strict_lower = (lax.broadcasted_iota(jnp.int32, (LANES, LANES), 0)
                    > lax.broadcasted_iota(jnp.int32, (LANES, LANES), 1)).astype(BF16)
    ct = ct_ref[...]
    fn_col = fc_ref[...]
    scores = []
    for p_i in range(np_):
        lft = lf_refs[p_i][...]
        hi, mid, lo = _split3(lft)
        gloc = _dot(hi, strict_lower) + _dot(mid, strict_lower) + _dot(lo, strict_lower)
        s = _dot(qbd, k_refs[p_i][...].astype(BF16))
        scores.append(s + tile_rows(gloc) + (fn_col + tile_rows(ct)))
        ct = ct + jnp.sum(lft, axis=-1, keepdims=True)
    ct_ref[...] = ct
    m_old = m_ref[...]
    m_new = m_old
    for s in scores:
        m_new = jnp.maximum(m_new, jnp.max(s, axis=-1, keepdims=True))
    alpha = jnp.exp(m_old - m_new)
    l_new = alpha * l_ref[...]
    acc = alpha * acc_ref[...]
    for p_i in range(np_):
        p = jnp.exp(scores[p_i] - m_new)
        l_new = l_new + jnp.sum(p, axis=-1, keepdims=True)
        acc = acc + _dot_nt(p.astype(BF16), v_refs[p_i][...].astype(BF16))
    m_ref[...] = m_new
    l_ref[...] = l_new
    acc_ref[...] = acc

    @pl.when(s_idx == pl.num_programs(1) - 1)
    def _():
        o = jnp.where(own_head(), acc * (1.0 / l_new), 0.0)
        for t in range(n_new):
            o_ref[t:t + 1, :] = jnp.sum(o[t * N_HEADS:(t + 1) * N_HEADS, :], axis=0, keepdims=True)


def _decode_attn(page_table, q, kn, vn, lfn_t, cache_kt, cache_vt, cache_lf_t, n_pages_step):
    db, n_new, d = q.shape
    n_pages = page_table.shape[1]
    page = cache_kt.shape[2]
    np_ = n_pages_step
    assert n_pages % np_ == 0 and page == LANES
    steps = n_pages // np_
    rows = n_new * N_HEADS

    def page_map(p_i):
        return lambda b, s, pt: (pt[b, n_pages - 1 - (s * np_ + p_i)], 0, 0)

    new_spec = pl.BlockSpec((None, n_new, d), lambda b, s, pt: (b, 0, 0))
    in_specs = ([new_spec, new_spec, new_spec,
                 pl.BlockSpec((None, N_HEADS, LANES), lambda b, s, pt: (b, 0, 0))]
                + [pl.BlockSpec((None, d, page), page_map(p_i)) for p_i in range(np_)]
                + [pl.BlockSpec((None, d, page), page_map(p_i)) for p_i in range(np_)]
                + [pl.BlockSpec((None, N_HEADS, page), page_map(p_i)) for p_i in range(np_)])
    grid_spec = pltpu.PrefetchScalarGridSpec(
        num_scalar_prefetch=1,
        grid=(db, steps),
        in_specs=in_specs,
        out_specs=pl.BlockSpec((None, n_new, d), lambda b, s, pt: (b, 0, 0)),
        scratch_shapes=[pltpu.VMEM((rows, d), BF16),
                        pltpu.VMEM((rows, 1), F32),
                        pltpu.VMEM((N_HEADS, 1), F32),
                        pltpu.VMEM((rows, 1), F32),
                        pltpu.VMEM((rows, 1), F32),
                        pltpu.VMEM((rows, d), F32),
                        pltpu.VMEM((LANES, d), F32),
                        pltpu.VMEM((LANES, d), F32)])
    return pl.pallas_call(
        functools.partial(_decode_attn_kernel, n_pages_step=np_, n_new=n_new),
        grid_spec=grid_spec,
        out_shape=jax.ShapeDtypeStruct((db, n_new, d), F32),
        compiler_params=_params(("parallel", "arbitrary")),
        name="decode_attn",
    )(page_table, q, kn, vn, lfn_t, *([cache_kt] * np_), *([cache_vt] * np_), *([cache_lf_t] * np_))


def _conv_core(x_ref, gpre_ref, wb_ref, wc_ref, wh_ref):
    hn = _rms(x_ref[...], gpre_ref[...]).astype(BF16)
    return _dot(hn, wb_ref[...]), _dot(hn, wc_ref[...]) * _dot(hn, wh_ref[...])


def _conv_finish(x_ref, gate, conv, wo_ref, gpost_ref, y_ref):
    m = _dot((gate * conv).astype(BF16), wo_ref[...])
    y_ref[...] = x_ref[...] + _rms(m, gpost_ref[...])


def _conv_prompt_kernel(x_ref, gpre_ref, wb_ref, wc_ref, wh_ref, ck_ref, wo_ref, gpost_ref,
                        y_ref, tail_ref, u_ref, gc_ref, *, tiles_per_batch, pad_rows):
    i = pl.program_id(0)
    tm = x_ref.shape[0]
    hist = SUBLANES

    @pl.when(i == 0)
    def _():
        u_ref[0:hist, :] = jnp.zeros((hist, u_ref.shape[1]), F32)

    @pl.when(i > 0)
    def _():
        u_ref[0:hist, :] = u_ref[tm:tm + hist, :]

    hn = _rms(x_ref[...], gpre_ref[...]).astype(BF16)
    last = i % tiles_per_batch == tiles_per_batch - 1
    n_real = jnp.where(last, tm - pad_rows, tm)
    real = lax.broadcasted_iota(jnp.int32, (tm, 1), 0) < n_real
    gw = 2 * LANES
    for cg in range(x_ref.shape[1] // gw):
        cols = slice(cg * gw, (cg + 1) * gw)
        gate = _dot(hn, wb_ref[:, cols])
        u = jnp.where(real, _dot(hn, wc_ref[:, cols]) * _dot(hn, wh_ref[:, cols]), 0.0)
        u_ref[hist:hist + tm, cols] = u
        tail_ref[0, :, cols] = u[tm - pad_rows - hist:tm - pad_rows, :]
        ck = ck_ref[:, cols]
        conv = (ck[0:1, :] * u_ref[hist - 2:hist - 2 + tm, cols]
                + ck[1:2, :] * u_ref[hist - 1:hist - 1 + tm, cols]
                + ck[2:3, :] * u)
        gc_ref[:, cols] = (gate * conv).astype(BF16)
    y_ref[...] = x_ref[...] + _rms(_dot(gc_ref[...], wo_ref[...]), gpost_ref[...])


def _conv_sample_kernel(x_ref, gpre_ref, wb_ref, wc_ref, wh_ref, ck_ref, wo_ref, gpost_ref,
                        h1_ref, h2_ref, y_ref, uo_ref, u_ref, *, n_new):
    tm = x_ref.shape[0]
    hist = SUBLANES
    gate, u = _conv_core(x_ref, gpre_ref, wb_ref, wc_ref, wh_ref)
    uo_ref[...] = u
    u_ref[0:hist, :] = jnp.zeros((hist, u_ref.shape[1]), F32)
    u_ref[hist:hist + tm, :] = u
    t = _mod_pow2(lax.broadcasted_iota(jnp.int32, (tm, 1), 0), n_new)
    ck = ck_ref[...]
    conv = (ck[0:1, :] * jnp.where(t >= 2, u_ref[hist - 2:hist - 2 + tm, :], h2_ref[...])
            + ck[1:2, :] * jnp.where(t >= 1, u_ref[hist - 1:hist - 1 + tm, :], h1_ref[...])
            + ck[2:3, :] * u)
    _conv_finish(x_ref, gate, conv, wo_ref, gpost_ref, y_ref)


def _conv_mixer(x, g_pre, wb, wc, wh, ck, wo, g_post, tm, tiles_per_batch=None, pad_rows=None, hist1=None,
                hist2=None, n_new=None):
    rows, d = x.shape
    row_spec = pl.BlockSpec((tm, d), lambda i: (i, 0))
    w_spec = _const_spec((d, d))
    in_specs = [row_spec, _const_spec((1, d)), w_spec, w_spec, w_spec, _const_spec((CONV_W, d)),
                w_spec, _const_spec((1, d))]
    f32_rows = jax.ShapeDtypeStruct((rows, d), F32)
    scratch = [pltpu.VMEM((tm + 2 * SUBLANES, d), F32)]
    if tiles_per_batch is None:
        return pl.pallas_call(
            functools.partial(_conv_sample_kernel, n_new=n_new),
            grid=(rows // tm,),
            in_specs=in_specs + [row_spec, row_spec],
            out_specs=[row_spec, row_spec],
            out_shape=[f32_rows, f32_rows],
            scratch_shapes=scratch,
            compiler_params=_params(("parallel",)),
            name="conv_mixer_sample",
        )(x, g_pre, wb, wc, wh, ck, wo, g_post, hist1, hist2)
    n_tiles = rows // tm
    assert pad_rows % SUBLANES == 0 and SUBLANES <= tm - pad_rows
    return pl.pallas_call(
        functools.partial(_conv_prompt_kernel, tiles_per_batch=tiles_per_batch, pad_rows=pad_rows),
        grid=(n_tiles,),
        in_specs=in_specs,
        out_specs=[row_spec, pl.BlockSpec((1, SUBLANES, d), lambda i: (i, 0, 0))],
        out_shape=[f32_rows, jax.ShapeDtypeStruct((n_tiles, SUBLANES, d), F32)],
        scratch_shapes=scratch + [pltpu.VMEM((tm, d), BF16)],
        compiler_params=_params(("arbitrary",)),
        name="conv_mixer_prompt",
    )(x, g_pre, wb, wc, wh, ck, wo, g_post)


def kernel(x_prompt, x_sample, cache_k, cache_v, cache_logf, state_conv, page_table, meta_tokens, norm_g,
           ffn_w_gate, ffn_w_up, ffn_w_down, attn_w_in, attn_b_f, attn_w_out, conv_w_in, conv_kernel,
           conv_w_out):
    n_batch, seq, d = x_prompt.shape
    db, n_new, _ = x_sample.shape
    depth = norm_g.shape[0]
    n_pool, page = cache_k.shape[1], cache_k.shape[2]
    l_real = N_META + seq
    pad_rows = (-l_real) % CHUNK
    lp = l_real + pad_rows
    tpb = lp // CHUNK
    tm_p = CHUNK
    tm_r = ROW_TILE if (n_batch * lp) % ROW_TILE == 0 else CHUNK
    tm_s = db * n_new
    assert d == D_MODEL and page == LANES and tm_s % SUBLANES == 0

    meta = jnp.broadcast_to(meta_tokens[None].astype(F32), (n_batch, N_META, d))
    xp = jnp.concatenate([meta, x_prompt, jnp.zeros((n_batch, pad_rows, d), F32)], axis=1).reshape(n_batch * lp, d)
    xs = x_sample.reshape(tm_s, d)

    def row(v):
        return v.reshape(1, -1).astype(F32)

    k_p, v_p, lf_p, cv_p, k_s, v_s, lf_s, cv_s = [], [], [], [], [], [], [], []
    for i in range(depth):
        g = norm_g[i]
        wg, wu, wd = (w[i].astype(BF16) for w in (ffn_w_gate, ffn_w_up, ffn_w_down))
        xp = _ffn(xp, row(g[0]), row(g[1]), wg[0], wu[0], wd[0], tm_r)
        xs = _ffn(xs, row(g[0]), row(g[1]), wg[0], wu[0], wd[0], tm_s)
        if i % 2 == 0:
            a = i // 2
            w_in = attn_w_in[a]
            wq, wk, wv = (w_in[:, j * d:(j + 1) * d].astype(BF16) for j in range(3))
            wf = jnp.pad(w_in[:, 3 * d:], ((0, 0), (0, LANES - N_HEADS))).astype(BF16)
            bf = jnp.pad(attn_b_f[a], (0, LANES - N_HEADS)).reshape(1, LANES).astype(F32)
            wo = attn_w_out[a].astype(BF16)

            kt, vt32, lfp, qa, ka, vt, cb = _attn_proj(xp, row(g[2]), wq, wk, wv, wf, bf, tm_p, tpb, l_real)
            cb = cb[:, 0, :N_HEADS].reshape(n_batch, tpb, N_HEADS)
            cb = cb.transpose(0, 2, 1).reshape(n_batch * N_HEADS * tpb)
            op = _prompt_attn(cb, qa, ka, vt, n_batch, lp)
            mix_p = (op, wo, row(g[3]))

            qs, ks, vs, lfs = _attn_proj(xs, row(g[2]), wq, wk, wv, wf, bf, tm_s)
            lfn_t = jnp.pad(lfs.reshape(db, n_new, N_HEADS).transpose(0, 2, 1),
                            ((0, 0), (0, 0), (0, LANES - n_new)))
            ck_t = cache_k[a].transpose(0, 2, 3, 1).reshape(n_pool, d, page)
            cv_t = cache_v[a].transpose(0, 2, 3, 1).reshape(n_pool, d, page)
            clf_t = cache_logf[a].astype(F32).transpose(0, 2, 1)
            osm = _decode_attn(page_table, qs.reshape(db, n_new, d), ks.reshape(db, n_new, d),
                               vs.reshape(db, n_new, d), lfn_t, ck_t, cv_t, clf_t, DECODE_PAGES_PER_STEP)
            mix_s = (osm.reshape(tm_s, d), wo, row(g[3]))

            k_p.append(kt.reshape(n_batch, N_HEADS, HEAD_DIM, l_real).transpose(0, 3, 1, 2))
            v_p.append(vt32.reshape(n_batch, N_HEADS, HEAD_DIM, l_real).transpose(0, 3, 1, 2))
            lf_p.append(lfp.reshape(n_batch, lp, N_HEADS)[:, :l_real])
            k_s.append(ks.reshape(db, n_new, N_HEADS, HEAD_DIM))
            v_s.append(vs.reshape(db, n_new, N_HEADS, HEAD_DIM))
            lf_s.append(lfs.reshape(db, n_new, N_HEADS))
        else:
            c = i // 2
            w_in = conv_w_in[c]
            wb, wc, wh = (w_in[:, j * d:(j + 1) * d].astype(BF16) for j in range(3))
            wo = conv_w_out[c].astype(BF16)
            ck = conv_kernel[c].astype(F32)
            xp, tails = _conv_mixer(xp, row(g[2]), wb, wc, wh, ck, wo, row(g[3]), tm_p, tiles_per_batch=tpb,
                                    pad_rows=pad_rows)
            tails = tails.reshape(n_batch, tpb, SUBLANES, d)
            cv_p.append(tails[:, tpb - 1, SUBLANES - (CONV_W - 1):])

            st = state_conv[c].astype(F32)
            zero = jnp.zeros((db, n_new, d), F32)
            hist1 = zero.at[:, 0].set(st[:, 1]).reshape(tm_s, d)
            hist2 = zero.at[:, 0].set(st[:, 0]).at[:, 1].set(st[:, 1]).reshape(tm_s, d)
            xs, us = _conv_mixer(xs, row(g[2]), wb, wc, wh, ck, wo, row(g[3]), tm_s, hist1=hist1,
                                 hist2=hist2, n_new=n_new)
            us_ext = jnp.concatenate([st, us.reshape(db, n_new, d)], axis=1)
            cv_s.append(us_ext[:, -(CONV_W - 1):])
            mix_p = mix_s = None
        final = i == depth - 1 and mix_p is None
        if final:
            tm_y = next(t for t in (ROW_TILE, CHUNK, BLK, SUBLANES) if seq % t == 0)
            y_prompt = _ffn(xp, row(g[4]), row(g[5]), wg[1], wu[1], wd[1], tm_y, window=(n_batch, N_META, seq))
        else:
            xp = _ffn(xp, row(g[4]), row(g[5]), wg[1], wu[1], wd[1], tm_r, mix_p)
        xs = _ffn(xs, row(g[4]), row(g[5]), wg[1], wu[1], wd[1], tm_s, mix_s)

    if not final:
        y_prompt = xp.reshape(n_batch, lp, d)[:, N_META:l_real]
    y_sample = xs.reshape(db, n_new, d)
    return (y_prompt, y_sample, jnp.stack(k_p), jnp.stack(v_p), jnp.stack(lf_p), jnp.stack(cv_p),
            jnp.stack(k_s), jnp.stack(v_s), jnp.stack(lf_s), jnp.stack(cv_s))
```
